```python
import jax
import jax.numpy as jnp
from jax import lax
import numpy as np

D_MODEL = 4096
BATCH = 2
SEQ = 8192
DEPTH = 4

CTX_LEN = 256
GRID_W = 64
N_EVEN = (DEPTH + 1) // 2
N_ODD = DEPTH // 2
D_FF = -(-(8 * D_MODEL) // (3 * 256)) * 256
MOD_RANK = D_MODEL // 8
FOURIER_WIDTH = D_MODEL // 2
FOURIER_GROUPS = 4
SGU_WIDTH = D_MODEL // 2
SGU_GROUPS = 4
CHUNK = 128
AB_IN_WIDTH = FOURIER_WIDTH + 2 * SGU_WIDTH
MIX_WIDTH = FOURIER_WIDTH + SGU_WIDTH
RWKV_HEAD = 64
RWKV_HEADS = D_MODEL // RWKV_HEAD
DECAY_LORA = max(32, int(round(1.8 * D_MODEL ** 0.5 / 32)) * 32)
AAA_LORA = max(32, int(round(1.8 * D_MODEL ** 0.5 / 32)) * 32)
GATE_LORA = max(32, int(round(0.6 * D_MODEL ** 0.8 / 32)) * 32)
RMS_EPS = 1e-6
LN_EPS = 1e-5
GN_EPS = 64e-5
POS_BASE = 10000.0

kernel_name = "hybrid_fourier_sgu_rwkv7_dit"


def rmsnorm(x, g):
    xf = x.astype(jnp.float32)
    y = xf * lax.rsqrt(jnp.mean(xf * xf, axis=-1, keepdims=True) + RMS_EPS)
    return (y * g.astype(jnp.float32)).astype(x.dtype)


def layernorm(x, g, b):
    xf = x.astype(jnp.float32)
    mu = jnp.mean(xf, axis=-1, keepdims=True)
    var = jnp.mean(jnp.square(xf - mu), axis=-1, keepdims=True)
    y = (xf - mu) * lax.rsqrt(var + LN_EPS)
    return (y * g.astype(jnp.float32) + b.astype(jnp.float32)).astype(x.dtype)


def modulate(h, shift, scale):
    return h * (1 + scale) + shift


def adaln(cond, w1, w2, b):
    m = (jax.nn.silu(cond) @ w1) @ w2 + b
    return jnp.split(m[:, None, :], 6, axis=-1)


def grid_pos_embed(length, dim):
    t = jnp.arange(length)
    row = (t // GRID_W).astype(jnp.float32)
    col = (t % GRID_W).astype(jnp.float32)
    quarter = dim // 4
    omega = POS_BASE ** (-jnp.arange(quarter, dtype=jnp.float32) / quarter)
    ang_r = row[:, None] * omega
    ang_c = col[:, None] * omega
    return jnp.concatenate([jnp.sin(ang_r), jnp.cos(ang_r), jnp.sin(ang_c), jnp.cos(ang_c)], axis=-1)


def swiglu(h, w1, w3, w2):
    return (jax.nn.silu(h @ w1) * (h @ w3)) @ w2


def fourier_sgu_mix(h, w_in, w_out, sgu_g, sgu_b, sgu_ws, sgu_bs):
    bsz, length, _ = h.shape
    p = h @ w_in
    pa = p[..., :FOURIER_WIDTH]
    pu = p[..., FOURIER_WIDTH:FOURIER_WIDTH + SGU_WIDTH]
    pv = p[..., FOURIER_WIDTH + SGU_WIDTH:]
    fa = pa.reshape(bsz, length, FOURIER_GROUPS, FOURIER_WIDTH // FOURIER_GROUPS).astype(jnp.float32)
    ya = jnp.real(jnp.fft.fftn(fa, axes=(1, 3), norm="ortho")).astype(h.dtype).reshape(bsz, length, FOURIER_WIDTH)
    u = jax.nn.gelu(pu)
    v = layernorm(jax.nn.gelu(pv), sgu_g, sgu_b)
    v = v.reshape(bsz, length // CHUNK, CHUNK, SGU_GROUPS, SGU_WIDTH // SGU_GROUPS)
    s = jnp.einsum("gpq,bnqgc->bnpgc", sgu_ws, v) + sgu_bs.T[None, None, :, :, None]
    yb = u * s.reshape(bsz, length, SGU_WIDTH)
    return jnp.concatenate([ya, yb], axis=-1) @ w_out


def grid_shift(x, rows):
    bsz, length, dim = x.shape
    g = x.reshape(bsz, rows, GRID_W, dim)
    q = dim // 4
    left = jnp.pad(g[:, :, :-1, :q], ((0, 0), (0, 0), (1, 0), (0, 0)))
    right = jnp.pad(g[:, :, 1:, q:2 * q], ((0, 0), (0, 0), (0, 1), (0, 0)))
    up = jnp.pad(g[:, :-1, :, 2 * q:3 * q], ((0, 0), (1, 0), (0, 0), (0, 0)))
    down = jnp.pad(g[:, 1:, :, 3 * q:], ((0, 0), (0, 1), (0, 0), (0, 0)))
    return jnp.concatenate([left, right, up, down], axis=-1).reshape(bsz, length, dim)


def seq_shift(x):
    half = x.shape[-1] // 2
    prev = jnp.pad(x[:, :-1, :half], ((0, 0), (1, 0), (0, 0)))
    nxt = jnp.pad(x[:, 1:, half:], ((0, 0), (0, 1), (0, 0)))
    return jnp.concatenate([prev, nxt], axis=-1)


def l2norm_heads(x):
    xf = x.astype(jnp.float32)
    return xf * lax.rsqrt(jnp.sum(xf * xf, axis=-1, keepdims=True) + 1e-12)


def rwkv_project(h, shifted, mu, w_rkv, w0, w1, w2, a0, a1, a2, g1, g2, k_k, k_a):
    bsz, length, _ = h.shape
    heads = lambda t: t.reshape(bsz, length, RWKV_HEADS, RWKV_HEAD)
    xx = shifted - h
    xr, xw, xk, xv, xa, xg = [h + xx * mu[i] for i in range(6)]
    r = heads(xr @ w_rkv[0])
    k = xk @ w_rkv[1]
    v = heads(xv @ w_rkv[2])
    g = jax.nn.sigmoid(xg @ g1) @ g2
    kk = l2norm_heads(heads(k * k_k))
    dirs = []
    for d in range(2):
        w_log = -jax.nn.softplus(-(w0[d] + jnp.tanh(xw @ w1[d]) @ w2[d])) - 0.5
        decay = jnp.exp(-jnp.exp(w_log.astype(jnp.float32)))
        a = jax.nn.sigmoid(a0[d] + (xa @ a1[d]) @ a2[d])
        k_d = heads(k * (1 + (a - 1) * k_a))
        dirs.append((heads(decay), k_d, -kk, kk * heads(a).astype(jnp.float32)))
    return r, v, g, dirs


def wkv_step(state, inp):
    r, w, k, v, a, b = inp
    sa = jnp.einsum("bhvk,bhk->bhv", state, a)
    state = state * w[:, :, None, :] + sa[..., None] * b[:, :, None, :] + v[..., None] * k[:, :, None, :]
    return state, jnp.einsum("bhvk,bhk->bhv", state, r)


def wkv_scan(s0, r, v, dir_inputs, reverse):
    decay, k, a, b = dir_inputs
    xs = tuple(jnp.swapaxes(t.astype(jnp.float32), 0, 1) for t in (r, decay, k, v, a, b))
    s_final, ys = lax.scan(wkv_step, s0, xs, reverse=reverse)
    return s_final, jnp.swapaxes(ys, 0, 1)


def rwkv_output(y, r, v, g, dirs, r_k, lnx_g, lnx_b, w_o):
    bsz, length, _, _ = y.shape
    mu = jnp.mean(y, axis=-1, keepdims=True)
    var = jnp.mean(jnp.square(y - mu), axis=-1, keepdims=True)
    yn = ((y - mu) * lax.rsqrt(var + GN_EPS)).reshape(bsz, length, D_MODEL)
    yn = yn * lnx_g.astype(jnp.float32) + lnx_b.astype(jnp.float32)
    k_bonus = 0.5 * (dirs[0][1] + dirs[1][1])
    bonus = jnp.sum((r * k_bonus * r_k).astype(jnp.float32), axis=-1, keepdims=True) * v.astype(jnp.float32)
    out = (yn + bonus.reshape(bsz, length, D_MODEL)).astype(g.dtype) * g
    return out @ w_o


def rwkv_time_mix(h_lat, h_ctx, rows, mu, w_rkv, w_o, w0, w1, w2, a0, a1, a2, g1, g2,
                  k_k, k_a, r_k, lnx_g, lnx_b, need_ctx_out):
    proj = (mu, w_rkv, w0, w1, w2, a0, a1, a2, g1, g2, k_k, k_a)
    lat_r, lat_v, lat_g, lat_dirs = rwkv_project(h_lat, grid_shift(h_lat, rows), *proj)
    ctx_r, ctx_v, ctx_g, ctx_dirs = rwkv_project(h_ctx, seq_shift(h_ctx), *proj)
    s0 = jnp.zeros((h_lat.shape[0], RWKV_HEADS, RWKV_HEAD, RWKV_HEAD), jnp.float32)
    y_lat, y_ctx = [], []
    for d, reverse in enumerate((False, True)):
        s_ctx, yc = wkv_scan(s0, ctx_r, ctx_v, ctx_dirs[d], reverse)
        _, yl = wkv_scan(s_ctx, lat_r, lat_v, lat_dirs[d], reverse)
        y_lat.append(yl)
        y_ctx.append(yc)
    o_lat = rwkv_output(y_lat[0] + y_lat[1], lat_r, lat_v, lat_g, lat_dirs, r_k, lnx_g, lnx_b, w_o)
    o_ctx = None
    if need_ctx_out:
        o_ctx = rwkv_output(y_ctx[0] + y_ctx[1], ctx_r, ctx_v, ctx_g, ctx_dirs, r_k, lnx_g, lnx_b, w_o)
    return o_lat, o_ctx


def setup_inputs(seed: int = 0) -> dict:
    key = jax.random.key(seed)
    ks = iter(jax.random.split(key, 40))
    f32 = jnp.float32
    D = D_MODEL

    def nrm(shape, scale):
        return jax.random.normal(next(ks), shape, f32) * scale

    def gain(shape):
        return 1.0 + nrm(shape, 0.05)

    return {
        "x": nrm((BATCH, SEQ, D), 1.0),
        "c": nrm((BATCH, D), 1.0),
        "ctx": nrm((BATCH, CTX_LEN, D), 1.0),
        "c_ctx": nrm((D,), 1.0),
        "mod_w1": nrm((DEPTH, D, MOD_RANK), D ** -0.5),
        "mod_w2": nrm((DEPTH, MOD_RANK, 6 * D), 0.5 * MOD_RANK ** -0.5),
        "mod_b": nrm((DEPTH, 6 * D), 0.05),
        "norm_g": gain((DEPTH, 4, D)),
        "ffn_w1": nrm((DEPTH, D, D_FF), D ** -0.5),
        "ffn_w3": nrm((DEPTH, D, D_FF), D ** -0.5),
        "ffn_w2": nrm((DEPTH, D_FF, D), D_FF ** -0.5),
        "ab_w_in": nrm((N_EVEN, D, AB_IN_WIDTH), D ** -0.5),
        "ab_w_out": nrm((N_EVEN, MIX_WIDTH, D), MIX_WIDTH ** -0.5),
        "sgu_ln_g": gain((N_EVEN, SGU_WIDTH)),
        "sgu_ln_b": nrm((N_EVEN, SGU_WIDTH), 0.02),
        "sgu_ws": nrm((N_EVEN, SGU_GROUPS, CHUNK, CHUNK), CHUNK ** -0.5),
        "sgu_bs": 1.0 + nrm((N_EVEN, SGU_GROUPS, CHUNK), 0.05),
        "rwkv_mu": jax.random.uniform(next(ks), (N_ODD, 6, D), f32),
        "rwkv_w_rkv": nrm((N_ODD, 3, D, D), D ** -0.5),
        "rwkv_w_o": nrm((N_ODD, D, D), D ** -0.5),
        "rwkv_w0": jax.random.uniform(next(ks), (N_ODD, 2, D), f32, -6.5, -1.5),
        "rwkv_w1": nrm((N_ODD, 2, D, DECAY_LORA), D ** -0.5),
        "rwkv_w2": nrm((N_ODD, 2, DECAY_LORA, D), 0.5 * DECAY_LORA ** -0.5),
        "rwkv_a0": nrm((N_ODD, 2, D), 0.1),
        "rwkv_a1": nrm((N_ODD, 2, D, AAA_LORA), D ** -0.5),
        "rwkv_a2": nrm((N_ODD, 2, AAA_LORA, D), AAA_LORA ** -0.5),
        "rwkv_g1": nrm((N_ODD, D, GATE_LORA), D ** -0.5),
        "rwkv_g2": nrm((N_ODD, GATE_LORA, D), GATE_LORA ** -0.5),
        "rwkv_k_k": 0.85 + nrm((N_ODD, D), 0.05),
        "rwkv_k_a": 1.0 + nrm((N_ODD, D), 0.05),
        "rwkv_r_k": nrm((N_ODD, RWKV_HEADS, RWKV_HEAD), 0.1),
        "rwkv_lnx_g": gain((N_ODD, D)),
        "rwkv_lnx_b": nrm((N_ODD, D), 0.02),
    }


def reference(x, c, ctx, c_ctx, mod_w1, mod_w2, mod_b, norm_g, ffn_w1, ffn_w3, ffn_w2,
              ab_w_in, ab_w_out, sgu_ln_g, sgu_ln_b, sgu_ws, sgu_bs,
              rwkv_mu, rwkv_w_rkv, rwkv_w_o, rwkv_w0, rwkv_w1, rwkv_w2, rwkv_a0, rwkv_a1, rwkv_a2,
              rwkv_g1, rwkv_g2, rwkv_k_k, rwkv_k_a, rwkv_r_k, rwkv_lnx_g, rwkv_lnx_b):
    length = x.shape[1]
    rows = length // GRID_W
    x = x + grid_pos_embed(length, x.shape[-1]).astype(x.dtype)
    z = ctx
    for layer in range(DEPTH):
        last = layer == DEPTH - 1
        i = layer // 2
        g = norm_g[layer]
        m_lat = adaln(c, mod_w1[layer], mod_w2[layer], mod_b[layer])
        m_ctx = adaln(c_ctx[None, :], mod_w1[layer], mod_w2[layer], mod_b[layer])
        h_lat = modulate(rmsnorm(x, g[0]), m_lat[0], m_lat[1])
        o_ctx = None
        if layer % 2 == 0:
            ab = (ab_w_in[i], ab_w_out[i], sgu_ln_g[i], sgu_ln_b[i], sgu_ws[i], sgu_bs[i])
            o_lat = fourier_sgu_mix(h_lat, *ab)
            if not last:
                o_ctx = fourier_sgu_mix(modulate(rmsnorm(z, g[0]), m_ctx[0], m_ctx[1]), *ab)
        else:
            h_ctx = modulate(rmsnorm(z, g[0]), m_ctx[0], m_ctx[1])
            o_lat, o_ctx = rwkv_time_mix(
                h_lat, h_ctx, rows, rwkv_mu[i], rwkv_w_rkv[i], rwkv_w_o[i], rwkv_w0[i], rwkv_w1[i],
                rwkv_w2[i], rwkv_a0[i], rwkv_a1[i], rwkv_a2[i], rwkv_g1[i], rwkv_g2[i], rwkv_k_k[i],
                rwkv_k_a[i], rwkv_r_k[i], rwkv_lnx_g[i], rwkv_lnx_b[i], need_ctx_out=not last)
        x = x + m_lat[2] * rmsnorm(o_lat, g[1])
        f_lat = swiglu(modulate(rmsnorm(x, g[2]), m_lat[3], m_lat[4]), ffn_w1[layer], ffn_w3[layer], ffn_w2[layer])
        x = x + m_lat[5] * rmsnorm(f_lat, g[3])
        if not last:
            z = z + m_ctx[2] * rmsnorm(o_ctx, g[1])
            f_ctx = swiglu(modulate(rmsnorm(z, g[2]), m_ctx[3], m_ctx[4]), ffn_w1[layer], ffn_w3[layer], ffn_w2[layer])
            z = z + m_ctx[5] * rmsnorm(f_ctx, g[3])
    return x
```

```python
import functools
import math

import jax
import jax.numpy as jnp
from jax import lax
from jax.experimental import pallas as pl
from jax.experimental.pallas import tpu as pltpu

F32 = jnp.float32
BF16 = jnp.bfloat16

GRID_W = 64
SGU_CHUNK = 128
MIX_GROUPS = 4
HEAD = 64
RMS_EPS, LN_EPS, GN_EPS, L2_EPS = 1e-6, 1e-5, 64e-5, 1e-12
POS_BASE = 10000.0
DECAY_SCALE = math.exp(-0.5)

LANES = 128
MXU_DIM = 256
SCAN_CHUNK = 64
PACK = MXU_DIM
HEADS_PER_PACK = PACK // HEAD
VMEM_LIMIT_BYTES = 48 * 1024 * 1024


def _params(*sem):
    return pltpu.CompilerParams(dimension_semantics=sem, vmem_limit_bytes=VMEM_LIMIT_BYTES)


def _fit(n, t):
    t = min(t, n)
    while n % t:
        t -= LANES
    return t


def _dot(a, b):
    return jnp.dot(a, b, preferred_element_type=F32)


def _dot_nt(a, b):
    return lax.dot_general(a, b, (((1,), (1,)), ((), ())), preferred_element_type=F32)


def _act(x, act):
    if act == "tanh":
        return jnp.tanh(x)
    if act == "sigmoid":
        return jax.nn.sigmoid(x)
    assert act is None
    return x


def _gelu(x):
    return 0.5 * x * (1.0 + jnp.tanh(math.sqrt(2.0 / math.pi) * (x + 0.044715 * (x * x * x))))


def _rms(x, g):
    return x * lax.rsqrt(jnp.mean(x * x, axis=-1, keepdims=True) + RMS_EPS) * g


def _mm_kernel(*refs, npair, nk, kaxis, act, scale):
    o_ref = refs[2 * npair]
    part = None
    for a_ref, b_ref in zip(refs[:npair], refs[npair:2 * npair]):
        d = _dot(a_ref[...].astype(BF16), b_ref[...].astype(BF16))
        part = d if part is None else part + d

    def finish(acc):
        if scale is not None:
            acc = acc * scale
        o_ref[...] = _act(acc, act).astype(o_ref.dtype)

    if nk == 1:
        finish(part)
        return
    acc_ref = refs[2 * npair + 1]
    k = pl.program_id(kaxis)

    @pl.when(k == 0)
    def _():
        acc_ref[...] = part

    @pl.when(k > 0)
    def _():
        acc_ref[...] += part

    @pl.when(k == nk - 1)
    def _():
        finish(acc_ref[...])


def matmul(a_list, b_list, *, tm, tn, tk=None, out_dtype=F32, act=None, scale=None):
    m, kdim = a_list[0].shape
    n = b_list[0].shape[1]
    tk = kdim if tk is None else tk
    tm, tn = min(tm, m), _fit(n, tn)
    assert m % tm == 0 and n % tn == 0 and kdim % tk == 0, (m, n, kdim, tm, tn, tk)
    nk = kdim // tk
    npair = len(a_list)
    kern = functools.partial(_mm_kernel, npair=npair, nk=nk, kaxis=2, act=act, scale=scale)
    return pl.pallas_call(
        kern,
        out_shape=jax.ShapeDtypeStruct((m, n), out_dtype),
        grid=(m // tm, n // tn, nk),
        in_specs=[pl.BlockSpec((tm, tk), lambda i, j, k: (i, k)) for _ in a_list]
        + [pl.BlockSpec((tk, tn), lambda i, j, k: (k, j)) for _ in b_list],
        out_specs=pl.BlockSpec((tm, tn), lambda i, j, k: (i, j)),
        scratch_shapes=[pltpu.VMEM((tm, tn), F32)] if nk > 1 else [],
        compiler_params=_params("parallel", "parallel", "arbitrary"),
    )(*a_list, *b_list)


def dft_positions(u, v, cmat, smat_neg, *, nbatch, length, row0, scale, tm, tn, tk):
    width = u.shape[1]
    tm, tk, tn = min(tm, length), min(tk, length), min(tn, width)
    assert length % tm == 0 and length % tk == 0 and width % tn == 0 and row0 % tk == 0
    nk, ni = length // tk, length // tm
    kern = functools.partial(_mm_kernel, npair=2, nk=nk, kaxis=3, act=None, scale=scale)
    a_spec = pl.BlockSpec((tm, tk), lambda b, i, j, k: (i, k))
    b_spec = pl.BlockSpec((tk, tn), lambda b, i, j, k: (row0 // tk + b * nk + k, j))
    return pl.pallas_call(
        kern,
        out_shape=jax.ShapeDtypeStruct((nbatch * length, width), BF16),
        grid=(nbatch, ni, width // tn, nk),
        in_specs=[a_spec, a_spec, b_spec, b_spec],
        out_specs=pl.BlockSpec((tm, tn), lambda b, i, j, k: (b * ni + i, j)),
        scratch_shapes=[pltpu.VMEM((tm, tn), F32)] if nk > 1 else [],
        compiler_params=_params("parallel", "parallel", "parallel", "arbitrary"),
    )(cmat, smat_neg, u, v)


def _swiglu_kernel(h_ref, w1_ref, w3_ref, o_ref):
    h = h_ref[...]
    a = _dot(h, w1_ref[...])
    b = _dot(h, w3_ref[...])
    o_ref[...] = (a * jax.nn.sigmoid(a) * b).astype(o_ref.dtype)


def swiglu_up(h, w1, w3, *, tm, tn):
    m, kdim = h.shape
    n = w1.shape[1]
    tm, tn = min(tm, m), min(tn, n)
    assert m % tm == 0 and n % tn == 0
    w_spec = pl.BlockSpec((kdim, tn), lambda i, j: (0, j))
    return pl.pallas_call(
        _swiglu_kernel,
        out_shape=jax.ShapeDtypeStruct((m, n), BF16),
        grid=(m // tm, n // tn),
        in_specs=[pl.BlockSpec((tm, kdim), lambda i, j: (i, 0)), w_spec, w_spec],
        out_specs=pl.BlockSpec((tm, tn), lambda i, j: (i, j)),
        compiler_params=_params("parallel", "parallel"),
    )(h, w1, w3)


def _group_map(tr, seq, nbatch):
    return lambda i: (jnp.minimum((i * tr) // seq, nbatch), 0, 0)


def _embed_kernel(x_ref, ctx_ref, pos_ref, g_ref, mod_ref, xo_ref, h_ref, *, n_lat_tiles):
    is_lat = pl.program_id(0) < n_lat_tiles
    x0 = jnp.where(is_lat, x_ref[...] + pos_ref[...], ctx_ref[...])
    xo_ref[...] = x0
    h = _rms(x0, g_ref[...]) * (1.0 + mod_ref[1:2, :]) + mod_ref[0:1, :]
    h_ref[...] = h.astype(h_ref.dtype)


def embed_and_norm(x2, ctx2, pos, gain, mods, *, tr, seq, nbatch, h_dtype):
    n_lat, d = x2.shape
    n_ctx = ctx2.shape[0]
    t = n_lat + n_ctx
    assert n_lat % tr == 0 and n_ctx % tr == 0 and seq % tr == 0
    nl, npos = n_lat // tr, seq // tr
    row = lambda f: pl.BlockSpec((tr, d), f)
    return pl.pallas_call(
        functools.partial(_embed_kernel, n_lat_tiles=nl),
        out_shape=(jax.ShapeDtypeStruct((t, d), F32), jax.ShapeDtypeStruct((t, d), h_dtype)),
        grid=(t // tr,),
        in_specs=[row(lambda i: (jnp.minimum(i, nl - 1), 0)),
                  row(lambda i: (jnp.maximum(i - nl, 0), 0)),
                  row(lambda i: (i % npos, 0)),
                  pl.BlockSpec((None, 1, d), lambda i: (0, 0, 0)),
                  pl.BlockSpec((None, 8, d), _group_map(tr, seq, nbatch))],
        out_specs=(row(lambda i: (i, 0)), row(lambda i: (i, 0))),
        compiler_params=_params("parallel"),
    )(x2, ctx2, pos, gain, mods)


def _resid_kernel(*refs, gate_row, with_h):
    if with_h:
        x_ref, o_ref, go_ref, modc_ref, gn_ref, modn_ref, xo_ref, h_ref = refs
    else:
        x_ref, o_ref, go_ref, modc_ref, xo_ref = refs
    xn = x_ref[...] + modc_ref[gate_row:gate_row + 1, :] * _rms(o_ref[...], go_ref[...])
    xo_ref[...] = xn
    if with_h:
        shift_row = 3 if gate_row == 2 else 0
        h = _rms(xn, gn_ref[...]) * (1.0 + modn_ref[shift_row + 1:shift_row + 2, :]) \
            + modn_ref[shift_row:shift_row + 1, :]
        h_ref[...] = h.astype(h_ref.dtype)


def gated_residual(x, o, gain_o, mods_cur, gate_row, *, tr, seq, nbatch, rows=None,
                   gain_next=None, mods_next=None, h_dtype=None):
    t, d = x.shape
    rows = t if rows is None else rows
    assert rows % tr == 0
    with_h = gain_next is not None
    row = pl.BlockSpec((tr, d), lambda i: (i, 0))
    gain = pl.BlockSpec((None, 1, d), lambda i: (0, 0, 0))
    mod = pl.BlockSpec((None, 8, d), _group_map(tr, seq, nbatch))
    ins, in_specs = [x, o, gain_o, mods_cur], [row, row, gain, mod]
    out_shape, out_specs = [jax.ShapeDtypeStruct((rows, d), F32)], [row]
    if with_h:
        ins += [gain_next, mods_next]
        in_specs += [gain, mod]
        out_shape.append(jax.ShapeDtypeStruct((rows, d), h_dtype))
        out_specs.append(row)
    res = pl.pallas_call(
        functools.partial(_resid_kernel, gate_row=gate_row, with_h=with_h),
        out_shape=tuple(out_shape),
        grid=(rows // tr,),
        in_specs=in_specs,
        out_specs=tuple(out_specs),
        compiler_params=_params("parallel"),
    )(*ins)
    return res if with_h else res[0]


def _abmid_kernel(p_ref, cs_ref, lng_ref, lnb_ref, ws_ref, bs_ref, u_ref, v_ref, yb_ref, *, fw, sw):
    gw, sg = fw // MIX_GROUPS, sw // MIX_GROUPS
    cs = cs_ref[...]
    for g in range(MIX_GROUPS):
        uv = _dot(p_ref[:, g * gw:(g + 1) * gw].astype(BF16), cs)
        u_ref[:, g * gw:(g + 1) * gw] = uv[:, :gw].astype(BF16)
        v_ref[:, g * gw:(g + 1) * gw] = uv[:, gw:].astype(BF16)
    u = _gelu(p_ref[:, fw:fw + sw])
    gv = _gelu(p_ref[:, fw + sw:fw + 2 * sw])
    mu = jnp.mean(gv, axis=-1, keepdims=True)
    dv = gv - mu
    var = jnp.mean(dv * dv, axis=-1, keepdims=True)
    vn = (dv * lax.rsqrt(var + LN_EPS) * lng_ref[...] + lnb_ref[...]).astype(BF16)
    for g in range(MIX_GROUPS):
        s = _dot(ws_ref[g].astype(BF16), vn[:, g * sg:(g + 1) * sg]) + bs_ref[:, g:g + 1]
        yb_ref[:, g * sg:(g + 1) * sg] = (u[:, g * sg:(g + 1) * sg] * s).astype(BF16)


def ab_mid(p, cs, ln_g, ln_b, ws, bs_t, *, fw, sw):
    t = p.shape[0]
    assert t % SGU_CHUNK == 0
    full = lambda a: pl.BlockSpec(a.shape, lambda i: (0,) * a.ndim)
    out = lambda w: pl.BlockSpec((SGU_CHUNK, w), lambda i: (i, 0))
    return pl.pallas_call(
        functools.partial(_abmid_kernel, fw=fw, sw=sw),
        out_shape=(jax.ShapeDtypeStruct((t, fw), BF16), jax.ShapeDtypeStruct((t, fw), BF16),
                   jax.ShapeDtypeStruct((t, sw), BF16)),
        grid=(t // SGU_CHUNK,),
        in_specs=[pl.BlockSpec((SGU_CHUNK, p.shape[1]), lambda i: (i, 0)),
                  full(cs), full(ln_g), full(ln_b), full(ws), full(bs_t)],
        out_specs=(out(fw), out(fw), out(sw)),
        compiler_params=_params("parallel"),
    )(p, cs, ln_g, ln_b, ws, bs_t)


def _dft_cos_sin(n):
    blk = min(n, LANES)
    nb = n // blk
    k = jnp.arange(n, dtype=jnp.int32)[:, None]
    w = 2.0 * math.pi / n
    ang_t = w * ((k * jnp.arange(blk, dtype=jnp.int32)[None, :]) % n).astype(F32)
    ang_p = w * ((k * (jnp.arange(nb, dtype=jnp.int32) * blk)[None, :]) % n).astype(F32)
    ct, st = jnp.cos(ang_t)[:, None, :], jnp.sin(ang_t)[:, None, :]
    cp, sp = jnp.cos(ang_p)[:, :, None], jnp.sin(ang_p)[:, :, None]
    return (cp * ct - sp * st).reshape(n, n), (sp * ct + cp * st).reshape(n, n)


def _shiftmix_kernel(cur_ref, up_ref, dn_ref, mu_ref, *out_refs, tr, n_lat_tiles, lat_tiles_per_seq,
                     ctx_tiles_per_seq, d):
    i = pl.program_id(0)
    rows = lax.broadcasted_iota(jnp.int32, (tr, 1), 0)

    def emit(lo, hi, shifted):
        h = cur_ref[:, lo:hi]
        xx = shifted - h
        for m, o_ref in enumerate(out_refs):
            o_ref[:, lo:hi] = (h + xx * mu_ref[m:m + 1, lo:hi]).astype(o_ref.dtype)

    @pl.when(i < n_lat_tiles)
    def _():
        q = d // 4
        col = rows & (GRID_W - 1)
        ib = i % lat_tiles_per_seq
        up_ok = (ib > 0).astype(F32)
        dn_ok = (ib < lat_tiles_per_seq - 1).astype(F32)
        emit(0, q, jnp.where(col > 0, pltpu.roll(cur_ref[:, 0:q], 1, 0), 0.0))
        emit(q, 2 * q, jnp.where(col < GRID_W - 1, pltpu.roll(cur_ref[:, q:2 * q], tr - 1, 0), 0.0))
        emit(2 * q, 3 * q, jnp.concatenate(
            [up_ref[:, 2 * q:3 * q] * up_ok, cur_ref[0:tr - GRID_W, 2 * q:3 * q]], axis=0))
        emit(3 * q, d, jnp.concatenate(
            [cur_ref[GRID_W:tr, 3 * q:d], dn_ref[:, 3 * q:d] * dn_ok], axis=0))

    @pl.when(i >= n_lat_tiles)
    def _():
        hf = d // 2
        jb = (i - n_lat_tiles) % ctx_tiles_per_seq
        up_ok = (jb > 0).astype(F32)
        dn_ok = (jb < ctx_tiles_per_seq - 1).astype(F32)
        prev = jnp.where(rows == 0, up_ref[GRID_W - 1:GRID_W, 0:hf] * up_ok,
                         pltpu.roll(cur_ref[:, 0:hf], 1, 0))
        nxt = jnp.where(rows == tr - 1, dn_ref[0:1, hf:d] * dn_ok,
                        pltpu.roll(cur_ref[:, hf:d], tr - 1, 0))
        emit(0, hf, prev)
        emit(hf, d, nxt)


def shift_mix(h, mu8, *, tr, n_lat, seq, ctx_len):
    t, d = h.shape
    assert t % tr == 0 and tr % GRID_W == 0 and seq % tr == 0 and ctx_len % tr == 0 and tr > GRID_W
    per = tr // GRID_W
    nhalo = t // GRID_W
    halo = lambda f: pl.BlockSpec((GRID_W, d), f)
    row = pl.BlockSpec((tr, d), lambda i: (i, 0))
    kern = functools.partial(_shiftmix_kernel, tr=tr, n_lat_tiles=n_lat // tr,
                             lat_tiles_per_seq=seq // tr, ctx_tiles_per_seq=ctx_len // tr, d=d)
    return pl.pallas_call(
        kern,
        out_shape=tuple(jax.ShapeDtypeStruct((t, d), BF16) for _ in range(6)),
        grid=(t // tr,),
        in_specs=[row,
                  halo(lambda i: (jnp.maximum(i * per - 1, 0), 0)),
                  halo(lambda i: (jnp.minimum((i + 1) * per, nhalo - 1), 0)),
                  pl.BlockSpec((8, d), lambda i: (0, 0))],
        out_specs=tuple(row for _ in range(6)),
        compiler_params=_params("parallel"),
    )(h, h, h, mu8)


def _headsum(x, bd):
    r, d = x.shape
    nt = d // LANES
    xs = jnp.concatenate([x[:, t * LANES:(t + 1) * LANES] for t in range(nt)], axis=0)
    hi = xs.astype(BF16)
    lo = (xs - hi.astype(F32)).astype(BF16)
    s = _dot(hi, bd) + _dot(lo, bd)
    return jnp.concatenate([s[t * r:(t + 1) * r] for t in range(nt)], axis=1)


def _prep_kernel(k_ref, hw_ref, ha_ref, w2_ref, a2_ref, w0_ref, a0_ref, kkp_ref, bd_ref,
                 kk_ref, lw0_ref, lw1_ref, as0_ref, as1_ref, *, lora):
    for dr, (lw_ref, as_ref) in enumerate(((lw0_ref, as0_ref), (lw1_ref, as1_ref))):
        z = _dot(hw_ref[:, dr * lora:(dr + 1) * lora].astype(BF16), w2_ref[dr]) + w0_ref[dr:dr + 1, :]
        lw_ref[...] = -DECAY_SCALE * jax.nn.sigmoid(z)
        za = _dot(ha_ref[:, dr * lora:(dr + 1) * lora].astype(BF16), a2_ref[dr]) + a0_ref[dr:dr + 1, :]
        as_ref[...] = jax.nn.sigmoid(za)
    kq = k_ref[...] * kkp_ref[...]
    kk_ref[...] = kq * lax.rsqrt(_headsum(kq * kq, bd_ref[...]) + L2_EPS)


def rwkv_prep(k, hw, ha, w2, a2, w0, a0, k_k, bd, *, tr, lora):
    t, d = k.shape
    assert t % tr == 0
    row = pl.BlockSpec((tr, d), lambda i: (i, 0))
    lrow = pl.BlockSpec((tr, 2 * lora), lambda i: (i, 0))
    full = lambda a: pl.BlockSpec(a.shape, lambda i: (0,) * a.ndim)
    return pl.pallas_call(
        functools.partial(_prep_kernel, lora=lora),
        out_shape=tuple(jax.ShapeDtypeStruct((t, d), F32) for _ in range(5)),
        grid=(t // tr,),
        in_specs=[row, lrow, lrow, full(w2), full(a2), full(w0), full(a0), full(k_k), full(bd)],
        out_specs=tuple(row for _ in range(5)),
        compiler_params=_params("parallel"),
    )(k, hw, ha, w2, a2, w0, a0, k_k, bd)


def _block_diag(x, mask):
    tiled = jnp.concatenate([x] * HEADS_PER_PACK, axis=0)
    return jnp.where(mask, tiled, 0.0).astype(BF16)


def _scan_kernel(r_ref, k_ref, v_ref, kk_ref, as_ref, lw_ref, ka_ref, y_ref, s_ref, *, reverse, npack):
    c = SCAN_CHUNK

    @pl.when(pl.program_id(2) == 0)
    def _():
        s_ref[...] = jnp.zeros_like(s_ref)

    def earlier(a, b, strict):
        if reverse:
            return (a > b) if strict else (a >= b)
        return (a < b) if strict else (a <= b)

    ti = lax.broadcasted_iota(jnp.int32, (c, c), 0)
    tj = lax.broadcasted_iota(jnp.int32, (c, c), 1)
    tri = jnp.where(earlier(tj, ti, False), 1.0, 0.0).astype(BF16)
    pt = lax.broadcasted_iota(jnp.int32, (c, PACK), 0)
    pj = lax.broadcasted_iota(jnp.int32, (c, PACK), 1) & (HEAD - 1)
    strict_m = earlier(pj, pt, True)
    incl_m = earlier(pj, pt, False)
    eye_m = pj == pt
    level_m = [((pt >> (l + 1)) == (pj >> (l + 1))) & ((pt >> l) != (pj >> l))
               for l in range(c.bit_length() - 1)]
    hshift = HEAD.bit_length() - 1
    br = lax.broadcasted_iota(jnp.int32, (PACK, PACK), 0) >> hshift
    bc = lax.broadcasted_iota(jnp.int32, (PACK, PACK), 1) >> hshift
    bmask = br == bc
    last = 0 if reverse else c - 1

    for q in range(npack):
        sl = slice(q * PACK, (q + 1) * PACK)
        lw = lw_ref[:, sl]
        hi = lw.astype(BF16)
        r1 = lw - hi.astype(F32)
        mid = r1.astype(BF16)
        lo = (r1 - mid.astype(F32)).astype(BF16)
        cl = _dot(tri, hi) + _dot(tri, mid) + _dot(tri, lo)
        tot = cl[last:last + 1, :]
        e_neg = jnp.exp(-cl)
        e_end = jnp.exp(tot - cl)
        asg = as_ref[:, sl]
        kkq = kk_ref[:, sl]
        kd = k_ref[:, sl] * (1.0 + (asg - 1.0) * ka_ref[:, sl])
        bq = kkq * asg
        vq = v_ref[:, sl]
        lhs = jnp.concatenate([-kkq * jnp.exp(cl - lw), r_ref[:, sl] * jnp.exp(cl)], axis=0).astype(BF16)
        mb = _dot_nt(lhs, _block_diag(bq * e_neg, bmask))
        mk = _dot_nt(lhs, _block_diag(kd * e_neg, bmask))
        s_old = s_ref[q]
        a_s = _dot_nt(lhs, s_old.astype(BF16))
        v_bd = _block_diag(vq, bmask)
        w_in = a_s[:c] + _dot(jnp.where(strict_m, mk[:c], 0.0).astype(BF16), v_bd)
        mab = jnp.where(strict_m, mb[:c], 0.0)
        tinv = jnp.where(eye_m, 1.0, 0.0) + jnp.where(level_m[0], mab, 0.0)
        for lm in level_m[1:]:
            dx = _dot(tinv.astype(BF16), _block_diag(jnp.where(lm, mab, 0.0), bmask))
            tinv = tinv + _dot(dx.astype(BF16), _block_diag(tinv, bmask))
        u = _dot(tinv.astype(BF16), _block_diag(w_in, bmask))
        y = a_s[c:] + _dot(jnp.where(incl_m, mb[c:], 0.0).astype(BF16), _block_diag(u, bmask)) \
            + _dot(jnp.where(incl_m, mk[c:], 0.0).astype(BF16), v_bd)
        y_ref[:, sl] = y
        uv_t = jnp.transpose(jnp.concatenate([u, vq], axis=0)).astype(BF16)
        bk = jnp.concatenate([bq * e_end, kd * e_end], axis=0).astype(BF16)
        s_ref[q] = s_old * jnp.exp(tot) + jnp.where(bmask, _dot(uv_t, bk), 0.0)


def wkv_scan(r, k, v, kk, a_sig, lw, k_a, *, reverse, nbatch, seq, ctx_len, lane_block):
    t, d = r.shape
    c = SCAN_CHUNK
    lane_block = min(lane_block, d)
    assert seq % c == 0 and ctx_len % c == 0 and d % lane_block == 0 and lane_block % PACK == 0
    n_lat_c, n_ctx_c = seq // c, ctx_len // c
    ctx0 = nbatch * n_lat_c

    def blk(b, hg, ci):
        cc = (n_ctx_c - 1 - ci) if reverse else ci
        lc = ci - n_ctx_c
        lc = (n_lat_c - 1 - lc) if reverse else lc
        return (jnp.where(ci < n_ctx_c, ctx0 + b * n_ctx_c + cc, b * n_lat_c + lc), hg)

    row = pl.BlockSpec((c, lane_block), blk)
    npack = lane_block // PACK
    return pl.pallas_call(
        functools.partial(_scan_kernel, reverse=reverse, npack=npack),
        out_shape=jax.ShapeDtypeStruct((t, d), F32),
        grid=(nbatch, d // lane_block, n_ctx_c + n_lat_c),
        in_specs=[row] * 6 + [pl.BlockSpec((1, lane_block), lambda b, hg, ci: (0, hg))],
        out_specs=row,
        scratch_shapes=[pltpu.VMEM((npack, PACK, PACK), F32)],
        compiler_params=_params("parallel", "parallel", "arbitrary"),
    )(r, k, v, kk, a_sig, lw, k_a)


def _rwkv_out_kernel(y0_ref, y1_ref, r_ref, k_ref, v_ref, as0_ref, as1_ref, g_ref,
                     ka_ref, rk_ref, lng_ref, lnb_ref, bd_ref, o_ref):
    bd = bd_ref[...]
    y = y0_ref[...] + y1_ref[...]
    inv = 1.0 / HEAD
    dy = y - _headsum(y, bd) * inv
    yn = dy * lax.rsqrt(_headsum(dy * dy, bd) * inv + GN_EPS) * lng_ref[...] + lnb_ref[...]
    k_bonus = k_ref[...] * (1.0 + (0.5 * (as0_ref[...] + as1_ref[...]) - 1.0) * ka_ref[...])
    bonus = _headsum(r_ref[...] * k_bonus * rk_ref[...], bd) * v_ref[...]
    o_ref[...] = ((yn + bonus) * g_ref[...]).astype(o_ref.dtype)


def rwkv_out(y0, y1, r, k, v, as0, as1, g, k_a, r_k, ln_g, ln_b, bd, *, tr):
    t, d = r.shape
    assert t % tr == 0
    row = pl.BlockSpec((tr, d), lambda i: (i, 0))
    full = lambda a: pl.BlockSpec(a.shape, lambda i: (0,) * a.ndim)
    return pl.pallas_call(
        _rwkv_out_kernel,
        out_shape=jax.ShapeDtypeStruct((t, d), BF16),
        grid=(t // tr,),
        in_specs=[row] * 8 + [full(k_a), full(r_k), full(ln_g), full(ln_b), full(bd)],
        out_specs=row,
        compiler_params=_params("parallel"),
    )(y0, y1, r, k, v, as0, as1, g, k_a, r_k, ln_g, ln_b, bd)


def _pad_to(a, axis, size):
    pad = [(0, 0)] * a.ndim
    pad[axis] = (0, size - a.shape[axis])
    return jnp.pad(a, pad)


def _round_up(n, m):
    return -(-n // m) * m


def kernel(x, c, ctx, c_ctx, mod_w1, mod_w2, mod_b, norm_g, ffn_w1, ffn_w3, ffn_w2, ab_w_in, ab_w_out, sgu_ln_g, sgu_ln_b, sgu_ws, sgu_bs, rwkv_mu, rwkv_w_rkv, rwkv_w_o, rwkv_w0, rwkv_w1, rwkv_w2, rwkv_a0, rwkv_a1, rwkv_a2, rwkv_g1, rwkv_g2, rwkv_k_k, rwkv_k_a, rwkv_r_k, rwkv_lnx_g, rwkv_lnx_b):
    nbatch, seq, d = x.shape
    ctx_len = ctx.shape[1]
    depth = mod_w1.shape[0]
    n_lat, n_ctx = nbatch * seq, nbatch * ctx_len
    sw = sgu_ln_g.shape[1]
    fw = ab_w_in.shape[2] - 2 * sw
    gw = fw // MIX_GROUPS
    lora = rwkv_w1.shape[3]
    tr = 256
    tm = 512
    common = dict(tr=tr, seq=seq, nbatch=nbatch)

    cond = _pad_to(jax.nn.silu(jnp.concatenate([c, c_ctx[None, :]], axis=0)), 0, 8)
    mods = []
    for layer in range(depth):
        hid = matmul([cond], [mod_w1[layer]], tm=8, tn=mod_w1.shape[2])
        m = matmul([hid], [mod_w2[layer]], tm=8, tn=2048) + mod_b[layer]
        mods.append(_pad_to(m[:nbatch + 1].reshape(nbatch + 1, 6, d), 1, 8))
    gains = norm_g.reshape(depth * 4, 1, d)
    gain = lambda layer, j: gains[layer * 4 + j:layer * 4 + j + 1]

    quarter = d // 4
    omega = POS_BASE ** (-jnp.arange(quarter, dtype=F32) / quarter)
    ang_r = jnp.arange(seq // GRID_W, dtype=F32)[:, None] * omega
    ang_c = jnp.arange(GRID_W, dtype=F32)[:, None] * omega
    pos = jnp.concatenate(
        [jnp.repeat(jnp.concatenate([jnp.sin(ang_r), jnp.cos(ang_r)], axis=-1), GRID_W, axis=0),
         jnp.tile(jnp.concatenate([jnp.sin(ang_c), jnp.cos(ang_c)], axis=-1), (seq // GRID_W, 1))], axis=-1)

    cc, sc = _dft_cos_sin(gw)
    cs_chan = jnp.concatenate([cc, sc], axis=1).astype(BF16)
    pos_dft = {}
    for length in (seq, ctx_len):
        cl_, sl_ = _dft_cos_sin(length)
        pos_dft[length] = (cl_.astype(BF16), (-sl_).astype(BF16))
    lane = jnp.arange(LANES)
    bd = (lane[:, None] // HEAD == lane[None, :] // HEAD).astype(BF16)

    dffp = _round_up(ffn_w1.shape[2], 1024)
    xs, h = embed_and_norm(x.reshape(n_lat, d), ctx.reshape(n_ctx, d), pos, gain(0, 0), mods[0],
                           h_dtype=BF16, **common)

    for layer in range(depth):
        i = layer // 2
        last = layer == depth - 1
        if layer % 2 == 0:
            p = matmul([h], [ab_w_in[i].astype(BF16)], tm=tm, tn=1024)
            u, v, yb = ab_mid(p, cs_chan, sgu_ln_g[i][None, :], sgu_ln_b[i][None, :], sgu_ws[i],
                              sgu_bs[i].T, fw=fw, sw=sw)
            ya = jnp.concatenate([
                dft_positions(u, v, *pos_dft[seq], nbatch=nbatch, length=seq, row0=0,
                              scale=1.0 / math.sqrt(seq * gw), tm=1024, tn=1024, tk=2048),
                dft_positions(u, v, *pos_dft[ctx_len], nbatch=nbatch, length=ctx_len, row0=n_lat,
                              scale=1.0 / math.sqrt(ctx_len * gw), tm=1024, tn=1024, tk=2048)], axis=0)
            w_out = ab_w_out[i].astype(BF16)
            o = matmul([ya, yb], [w_out[:fw], w_out[fw:]], tm=tm, tn=1024)
        else:
            mixes = shift_mix(h, _pad_to(rwkv_mu[i], 0, 8), tr=128, n_lat=n_lat, seq=seq, ctx_len=ctx_len)
            xr, xw, xk, xv, xa, xg = mixes
            w_rkv = rwkv_w_rkv[i].astype(BF16)
            r = matmul([xr], [w_rkv[0]], tm=tm, tn=1024)
            k = matmul([xk], [w_rkv[1]], tm=tm, tn=1024)
            v = matmul([xv], [w_rkv[2]], tm=tm, tn=1024)
            hw = matmul([xw], [jnp.concatenate([rwkv_w1[i, 0], rwkv_w1[i, 1]], axis=1).astype(BF16)],
                        tm=tm, tn=2 * lora, act="tanh")
            ha = matmul([xa], [jnp.concatenate([rwkv_a1[i, 0], rwkv_a1[i, 1]], axis=1).astype(BF16)],
                        tm=tm, tn=2 * lora)
            glp = _round_up(rwkv_g1.shape[2], LANES)
            hg = matmul([xg], [_pad_to(rwkv_g1[i], 1, glp).astype(BF16)], tm=tm, tn=glp,
                        act="sigmoid", out_dtype=BF16)
            g = matmul([hg], [_pad_to(rwkv_g2[i], 0, glp).astype(BF16)], tm=tm, tn=1024)
            k_a = rwkv_k_a[i][None, :]
            kk, lw0, lw1, as0, as1 = rwkv_prep(
                k, hw, ha, rwkv_w2[i].astype(BF16), rwkv_a2[i].astype(BF16), rwkv_w0[i], rwkv_a0[i],
                rwkv_k_k[i][None, :], bd, tr=128, lora=lora)
            scan = functools.partial(wkv_scan, r, k, v, kk, nbatch=nbatch, seq=seq, ctx_len=ctx_len,
                                     lane_block=1024)
            y0 = scan(as0, lw0, k_a, reverse=False)
            y1 = scan(as1, lw1, k_a, reverse=True)
            ob = rwkv_out(y0, y1, r, k, v, as0, as1, g, k_a, rwkv_r_k[i].reshape(1, d),
                          rwkv_lnx_g[i][None, :], rwkv_lnx_b[i][None, :], bd, tr=128)
            o = matmul([ob], [rwkv_w_o[i].astype(BF16)], tm=tm, tn=1024)

        xs, h = gated_residual(xs, o, gain(layer, 1), mods[layer], 2, gain_next=gain(layer, 2),
                               mods_next=mods[layer], h_dtype=BF16, **common)
        hid = swiglu_up(h, _pad_to(ffn_w1[layer], 1, dffp).astype(BF16),
                        _pad_to(ffn_w3[layer], 1, dffp).astype(BF16), tm=tm, tn=512)
        f = matmul([hid], [_pad_to(ffn_w2[layer], 0, dffp).astype(BF16)], tm=tm, tn=1024, tk=dffp // 4)
        if last:
            xs = gated_residual(xs, f, gain(layer, 3), mods[layer], 5, rows=n_lat, **common)
        else:
            xs, h = gated_residual(xs, f, gain(layer, 3), mods[layer], 5, gain_next=gain(layer + 1, 0),
                                   mods_next=mods[layer + 1],
                                   h_dtype=F32 if (layer + 1) % 2 else BF16, **common)
    return xs.reshape(nbatch, seq, d)
```

```python
import functools
import math

import jax
import jax.numpy as jnp
from jax import lax
from jax.experimental import pallas as pl
from jax.experimental.pallas import tpu as pltpu

F32 = jnp.float32
BF16 = jnp.bfloat16

GRID_W = 64
SGU_CHUNK = 128
MIX_GROUPS = 4
HEAD = 64
RMS_EPS, LN_EPS, GN_EPS, L2_EPS = 1e-6, 1e-5, 64e-5, 1e-12
POS_BASE = 10000.0
DECAY_SCALE = math.exp(-0.5)

LANES = 128
MXU_DIM = 256
SCAN_CHUNK = 64
PACK = MXU_DIM
HEADS_PER_PACK = PACK // HEAD
VMEM_LIMIT_BYTES = 48 * 1024 * 1024


def _params(*sem):
    return pltpu.CompilerParams(dimension_semantics=sem, vmem_limit_bytes=VMEM_LIMIT_BYTES)


def _fit(n, t):
    t = min(t, n)
    while n % t:
        t -= LANES
    return t


def _dot(a, b):
    return jnp.dot(a, b, preferred_element_type=F32)


def _dot_nt(a, b):
    return lax.dot_general(a, b, (((1,), (1,)), ((), ())), preferred_element_type=F32)


def _act(x, act):
    if act == "tanh":
        return jnp.tanh(x)
    if act == "sigmoid":
        return jax.nn.sigmoid(x)
    assert act is None
    return x


def _gelu(x):
    return 0.5 * x * (1.0 + jnp.tanh(math.sqrt(2.0 / math.pi) * (x + 0.044715 * (x * x * x))))


def _rms(x, g):
    return x * lax.rsqrt(jnp.mean(x * x, axis=-1, keepdims=True) + RMS_EPS) * g


def _mm_kernel(*refs, npair, nk, kaxis, act, scale):
    o_ref = refs[2 * npair]

    def product():
        part = None
        for a_ref, b_ref in zip(refs[:npair], refs[npair:2 * npair]):
            d = _dot(a_ref[...].astype(BF16), b_ref[...].astype(BF16))
            part = d if part is None else part + d
        return part

    def finish(acc):
        if scale is not None:
            acc = acc * scale
        o_ref[...] = _act(acc, act).astype(o_ref.dtype)

    if nk == 1:
        finish(product())
        return
    acc_ref = refs[2 * npair + 1]
    k = pl.program_id(kaxis)

    @pl.when(k == 0)
    def _():
        acc_ref[...] = jnp.zeros_like(acc_ref)

    acc_ref[...] += product()

    @pl.when(k == nk - 1)
    def _():
        finish(acc_ref[...])


def matmul(a_list, b_list, *, tm, tn, tk=None, out_dtype=F32, act=None, scale=None, name="mm"):
    m, kdim = a_list[0].shape
    n = b_list[0].shape[1]
    tk = kdim if tk is None else tk
    tm, tn = min(tm, m), _fit(n, tn)
    assert m % tm == 0 and n % tn == 0 and kdim % tk == 0, (m, n, kdim, tm, tn, tk)
    nk = kdim // tk
    npair = len(a_list)
    kern = functools.partial(_mm_kernel, npair=npair, nk=nk, kaxis=2, act=act, scale=scale)
    return pl.pallas_call(
        kern,
        name=name,
        out_shape=jax.ShapeDtypeStruct((m, n), out_dtype),
        grid=(m // tm, n // tn, nk),
        in_specs=[pl.BlockSpec((tm, tk), lambda i, j, k: (i, k)) for _ in a_list]
        + [pl.BlockSpec((tk, tn), lambda i, j, k: (k, j)) for _ in b_list],
        out_specs=pl.BlockSpec((tm, tn), lambda i, j, k: (i, j)),
        scratch_shapes=[pltpu.VMEM((tm, tn), F32)] if nk > 1 else [],
        compiler_params=_params("parallel", "parallel", "arbitrary"),
    )(*a_list, *b_list)


def dft_positions(u, v, cmat, smat_neg, *, nbatch, length, row0, scale, tm, tn, tk):
    width = u.shape[1]
    tm, tk, tn = min(tm, length), min(tk, length), min(tn, width)
    assert length % tm == 0 and length % tk == 0 and width % tn == 0 and row0 % tk == 0
    nk, ni = length // tk, length // tm
    kern = functools.partial(_mm_kernel, npair=2, nk=nk, kaxis=3, act=None, scale=scale)
    a_spec = pl.BlockSpec((tm, tk), lambda b, i, j, k: (i, k))
    b_spec = pl.BlockSpec((tk, tn), lambda b, i, j, k: (row0 // tk + b * nk + k, j))
    return pl.pallas_call(
        kern,
        name="dft_positions",
        out_shape=jax.ShapeDtypeStruct((nbatch * length, width), BF16),
        grid=(nbatch, ni, width // tn, nk),
        in_specs=[a_spec, a_spec, b_spec, b_spec],
        out_specs=pl.BlockSpec((tm, tn), lambda b, i, j, k: (b * ni + i, j)),
        scratch_shapes=[pltpu.VMEM((tm, tn), F32)] if nk > 1 else [],
        compiler_params=_params("parallel", "parallel", "parallel", "arbitrary"),
    )(cmat, smat_neg, u, v)


def _swiglu_kernel(h_ref, w1_ref, w3_ref, o_ref):
    h = h_ref[...]
    a = _dot(h, w1_ref[...])
    b = _dot(h, w3_ref[...])
    o_ref[...] = (a * jax.nn.sigmoid(a) * b).astype(o_ref.dtype)


def swiglu_up(h, w1, w3, *, tm, tn):
    m, kdim = h.shape
    n = w1.shape[1]
    tm, tn = min(tm, m), min(tn, n)
    assert m % tm == 0 and n % tn == 0
    w_spec = pl.BlockSpec((kdim, tn), lambda i, j: (0, j))
    return pl.pallas_call(
        _swiglu_kernel,
        name="swiglu_up",
        out_shape=jax.ShapeDtypeStruct((m, n), BF16),
        grid=(m // tm, n // tn),
        in_specs=[pl.BlockSpec((tm, kdim), lambda i, j: (i, 0)), w_spec, w_spec],
        out_specs=pl.BlockSpec((tm, tn), lambda i, j: (i, j)),
        compiler_params=_params("parallel", "parallel"),
    )(h, w1, w3)


def _group_map(tr, seq, nbatch):
    return lambda i: (jnp.minimum((i * tr) // seq, nbatch), 0, 0)


def _embed_kernel(x_ref, ctx_ref, pos_ref, g_ref, mod_ref, xo_ref, h_ref, *, n_lat_tiles):
    is_lat = pl.program_id(0) < n_lat_tiles
    x0 = jnp.where(is_lat, x_ref[...] + pos_ref[...], ctx_ref[...])
    xo_ref[...] = x0
    h = _rms(x0, g_ref[...]) * (1.0 + mod_ref[1:2, :]) + mod_ref[0:1, :]
    h_ref[...] = h.astype(h_ref.dtype)


def embed_and_norm(x2, ctx2, pos, gain, mods, *, tr, seq, nbatch, h_dtype):
    n_lat, d = x2.shape
    n_ctx = ctx2.shape[0]
    t = n_lat + n_ctx
    assert n_lat % tr == 0 and n_ctx % tr == 0 and seq % tr == 0
    nl, npos = n_lat // tr, seq // tr
    row = lambda f: pl.BlockSpec((tr, d), f)
    return pl.pallas_call(
        functools.partial(_embed_kernel, n_lat_tiles=nl),
        name="embed_norm",
        out_shape=(jax.ShapeDtypeStruct((t, d), F32), jax.ShapeDtypeStruct((t, d), h_dtype)),
        grid=(t // tr,),
        in_specs=[row(lambda i: (jnp.minimum(i, nl - 1), 0)),
                  row(lambda i: (jnp.maximum(i - nl, 0), 0)),
                  row(lambda i: (i % npos, 0)),
                  pl.BlockSpec((None, 1, d), lambda i: (0, 0, 0)),
                  pl.BlockSpec((None, 8, d), _group_map(tr, seq, nbatch))],
        out_specs=(row(lambda i: (i, 0)), row(lambda i: (i, 0))),
        compiler_params=_params("parallel"),
    )(x2, ctx2, pos, gain, mods)


def _resid_kernel(*refs, gate_row, with_h):
    if with_h:
        x_ref, o_ref, go_ref, modc_ref, gn_ref, modn_ref, xo_ref, h_ref = refs
    else:
        x_ref, o_ref, go_ref, modc_ref, xo_ref = refs
    xn = x_ref[...] + modc_ref[gate_row:gate_row + 1, :] * _rms(o_ref[...].astype(F32), go_ref[...])
    xo_ref[...] = xn
    if with_h:
        shift_row = 3 if gate_row == 2 else 0
        h = _rms(xn, gn_ref[...]) * (1.0 + modn_ref[shift_row + 1:shift_row + 2, :]) \
            + modn_ref[shift_row:shift_row + 1, :]
        h_ref[...] = h.astype(h_ref.dtype)


def gated_residual(x, o, gain_o, mods_cur, gate_row, *, tr, seq, nbatch, rows=None,
                   gain_next=None, mods_next=None, h_dtype=None):
    t, d = x.shape
    rows = t if rows is None else rows
    assert rows % tr == 0
    with_h = gain_next is not None
    row = pl.BlockSpec((tr, d), lambda i: (i, 0))
    gain = pl.BlockSpec((None, 1, d), lambda i: (0, 0, 0))
    mod = pl.BlockSpec((None, 8, d), _group_map(tr, seq, nbatch))
    ins, in_specs = [x, o, gain_o, mods_cur], [row, row, gain, mod]
    out_shape, out_specs = [jax.ShapeDtypeStruct((rows, d), F32)], [row]
    if with_h:
        ins += [gain_next, mods_next]
        in_specs += [gain, mod]
        out_shape.append(jax.ShapeDtypeStruct((rows, d), h_dtype))
        out_specs.append(row)
    res = pl.pallas_call(
        functools.partial(_resid_kernel, gate_row=gate_row, with_h=with_h),
        name="gated_residual",
        out_shape=tuple(out_shape),
        grid=(rows // tr,),
        in_specs=in_specs,
        out_specs=tuple(out_specs),
        compiler_params=_params("parallel"),
    )(*ins)
    return res if with_h else res[0]


def _abmid_kernel(p_ref, cs_ref, lng_ref, lnb_ref, ws_ref, bs_ref, u_ref, v_ref, yb_ref, *, fw, sw):
    gw, sg = fw // MIX_GROUPS, sw // MIX_GROUPS
    cs = cs_ref[...]
    for g in range(MIX_GROUPS):
        uv = _dot(p_ref[:, g * gw:(g + 1) * gw].astype(BF16), cs)
        u_ref[:, g * gw:(g + 1) * gw] = uv[:, :gw].astype(BF16)
        v_ref[:, g * gw:(g + 1) * gw] = uv[:, gw:].astype(BF16)
    u = _gelu(p_ref[:, fw:fw + sw])
    gv = _gelu(p_ref[:, fw + sw:fw + 2 * sw])
    mu = jnp.mean(gv, axis=-1, keepdims=True)
    dv = gv - mu
    var = jnp.mean(dv * dv, axis=-1, keepdims=True)
    vn = (dv * lax.rsqrt(var + LN_EPS) * lng_ref[...] + lnb_ref[...]).astype(BF16)
    for g in range(MIX_GROUPS):
        s = _dot(ws_ref[g].astype(BF16), vn[:, g * sg:(g + 1) * sg]) + bs_ref[:, g:g + 1]
        yb_ref[:, g * sg:(g + 1) * sg] = (u[:, g * sg:(g + 1) * sg] * s).astype(BF16)


def ab_mid(p, cs, ln_g, ln_b, ws, bs_t, *, fw, sw):
    t = p.shape[0]
    assert t % SGU_CHUNK == 0
    full = lambda a: pl.BlockSpec(a.shape, lambda i: (0,) * a.ndim)
    out = lambda w: pl.BlockSpec((SGU_CHUNK, w), lambda i: (i, 0))
    return pl.pallas_call(
        functools.partial(_abmid_kernel, fw=fw, sw=sw),
        name="ab_mid",
        out_shape=(jax.ShapeDtypeStruct((t, fw), BF16), jax.ShapeDtypeStruct((t, fw), BF16),
                   jax.ShapeDtypeStruct((t, sw), BF16)),
        grid=(t // SGU_CHUNK,),
        in_specs=[pl.BlockSpec((SGU_CHUNK, p.shape[1]), lambda i: (i, 0)),
                  full(cs), full(ln_g), full(ln_b), full(ws), full(bs_t)],
        out_specs=(out(fw), out(fw), out(sw)),
        compiler_params=_params("parallel"),
    )(p, cs, ln_g, ln_b, ws, bs_t)


def _dft_cos_sin(n):
    blk = min(n, LANES)
    nb = n // blk
    k = jnp.arange(n, dtype=jnp.int32)[:, None]
    w = 2.0 * math.pi / n
    ang_t = w * ((k * jnp.arange(blk, dtype=jnp.int32)[None, :]) % n).astype(F32)
    ang_p = w * ((k * (jnp.arange(nb, dtype=jnp.int32) * blk)[None, :]) % n).astype(F32)
    ct, st = jnp.cos(ang_t)[:, None, :], jnp.sin(ang_t)[:, None, :]
    cp, sp = jnp.cos(ang_p)[:, :, None], jnp.sin(ang_p)[:, :, None]
    return (cp * ct - sp * st).reshape(n, n), (sp * ct + cp * st).reshape(n, n)


def _shiftmix_kernel(cur_ref, up_ref, dn_ref, mu_ref, *out_refs, tr, n_lat_tiles, lat_tiles_per_seq,
                     ctx_tiles_per_seq, d):
    i = pl.program_id(0)
    rows = lax.broadcasted_iota(jnp.int32, (tr, 1), 0)

    def emit(lo, hi, shifted):
        h = cur_ref[:, lo:hi]
        xx = shifted - h
        for m, o_ref in enumerate(out_refs):
            o_ref[:, lo:hi] = (h + xx * mu_ref[m:m + 1, lo:hi]).astype(o_ref.dtype)

    @pl.when(i < n_lat_tiles)
    def _():
        q = d // 4
        col = rows & (GRID_W - 1)
        ib = i % lat_tiles_per_seq
        up_ok = (ib > 0).astype(F32)
        dn_ok = (ib < lat_tiles_per_seq - 1).astype(F32)
        emit(0, q, jnp.where(col > 0, pltpu.roll(cur_ref[:, 0:q], 1, 0), 0.0))
        emit(q, 2 * q, jnp.where(col < GRID_W - 1, pltpu.roll(cur_ref[:, q:2 * q], tr - 1, 0), 0.0))
        emit(2 * q, 3 * q, jnp.concatenate(
            [up_ref[:, 2 * q:3 * q] * up_ok, cur_ref[0:tr - GRID_W, 2 * q:3 * q]], axis=0))
        emit(3 * q, d, jnp.concatenate(
            [cur_ref[GRID_W:tr, 3 * q:d], dn_ref[:, 3 * q:d] * dn_ok], axis=0))

    @pl.when(i >= n_lat_tiles)
    def _():
        hf = d // 2
        jb = (i - n_lat_tiles) % ctx_tiles_per_seq
        up_ok = (jb > 0).astype(F32)
        dn_ok = (jb < ctx_tiles_per_seq - 1).astype(F32)
        prev = jnp.where(rows == 0, up_ref[GRID_W - 1:GRID_W, 0:hf] * up_ok,
                         pltpu.roll(cur_ref[:, 0:hf], 1, 0))
        nxt = jnp.where(rows == tr - 1, dn_ref[0:1, hf:d] * dn_ok,
                        pltpu.roll(cur_ref[:, hf:d], tr - 1, 0))
        emit(0, hf, prev)
        emit(hf, d, nxt)


def shift_mix(h, mu8, *, tr, n_lat, seq, ctx_len):
    t, d = h.shape
    assert t % tr == 0 and tr % GRID_W == 0 and seq % tr == 0 and ctx_len % tr == 0 and tr > GRID_W
    per = tr // GRID_W
    nhalo = t // GRID_W
    halo = lambda f: pl.BlockSpec((GRID_W, d), f)
    row = pl.BlockSpec((tr, d), lambda i: (i, 0))
    kern = functools.partial(_shiftmix_kernel, tr=tr, n_lat_tiles=n_lat // tr,
                             lat_tiles_per_seq=seq // tr, ctx_tiles_per_seq=ctx_len // tr, d=d)
    return pl.pallas_call(
        kern,
        name="shift_mix",
        out_shape=tuple(jax.ShapeDtypeStruct((t, d), BF16) for _ in range(6)),
        grid=(t // tr,),
        in_specs=[row,
                  halo(lambda i: (jnp.maximum(i * per - 1, 0), 0)),
                  halo(lambda i: (jnp.minimum((i + 1) * per, nhalo - 1), 0)),
                  pl.BlockSpec((8, d), lambda i: (0, 0))],
        out_specs=tuple(row for _ in range(6)),
        compiler_params=_params("parallel"),
    )(h, h, h, mu8)


def _headsum(x, bd):
    r, d = x.shape
    nt = d // LANES
    xs = jnp.concatenate([x[:, t * LANES:(t + 1) * LANES] for t in range(nt)], axis=0)
    hi = xs.astype(BF16)
    lo = (xs - hi.astype(F32)).astype(BF16)
    s = _dot(hi, bd) + _dot(lo, bd)
    return jnp.concatenate([s[t * r:(t + 1) * r] for t in range(nt)], axis=1)


def _prep_kernel(k_ref, hw_ref, ha_ref, w2_ref, a2_ref, w0_ref, a0_ref, kkp_ref, bd_ref,
                 kk_ref, lw0_ref, lw1_ref, as0_ref, as1_ref, *, lora):
    for dr, (lw_ref, as_ref) in enumerate(((lw0_ref, as0_ref), (lw1_ref, as1_ref))):
        z = _dot(hw_ref[:, dr * lora:(dr + 1) * lora].astype(BF16), w2_ref[dr]) + w0_ref[dr:dr + 1, :]
        lw_ref[...] = -DECAY_SCALE * jax.nn.sigmoid(z)
        za = _dot(ha_ref[:, dr * lora:(dr + 1) * lora].astype(BF16), a2_ref[dr]) + a0_ref[dr:dr + 1, :]
        as_ref[...] = jax.nn.sigmoid(za)
    kq = k_ref[...] * kkp_ref[...]
    kk_ref[...] = kq * lax.rsqrt(_headsum(kq * kq, bd_ref[...]) + L2_EPS)


def rwkv_prep(k, hw, ha, w2, a2, w0, a0, k_k, bd, *, tr, lora):
    t, d = k.shape
    assert t % tr == 0
    row = pl.BlockSpec((tr, d), lambda i: (i, 0))
    lrow = pl.BlockSpec((tr, 2 * lora), lambda i: (i, 0))
    full = lambda a: pl.BlockSpec(a.shape, lambda i: (0,) * a.ndim)
    return pl.pallas_call(
        functools.partial(_prep_kernel, lora=lora),
        name="rwkv_prep",
        out_shape=tuple(jax.ShapeDtypeStruct((t, d), F32) for _ in range(5)),
        grid=(t // tr,),
        in_specs=[row, lrow, lrow, full(w2), full(a2), full(w0), full(a0), full(k_k), full(bd)],
        out_specs=tuple(row for _ in range(5)),
        compiler_params=_params("parallel"),
    )(k, hw, ha, w2, a2, w0, a0, k_k, bd)


def _block_diag(x, mask):
    tiled = jnp.concatenate([x] * HEADS_PER_PACK, axis=0)
    return jnp.where(mask, tiled, 0.0).astype(BF16)


def _scan_kernel(r_ref, k_ref, v_ref, kk_ref, as_ref, lw_ref, ka_ref, y_ref, s_ref, *, reverse, npack):
    c = SCAN_CHUNK

    @pl.when(pl.program_id(2) == 0)
    def _():
        s_ref[...] = jnp.zeros_like(s_ref)

    def earlier(a, b, strict):
        if reverse:
            return (a > b) if strict else (a >= b)
        return (a < b) if strict else (a <= b)

    ti = lax.broadcasted_iota(jnp.int32, (c, c), 0)
    tj = lax.broadcasted_iota(jnp.int32, (c, c), 1)
    tri = jnp.where(earlier(tj, ti, False), 1.0, 0.0).astype(BF16)
    pt = lax.broadcasted_iota(jnp.int32, (c, PACK), 0)
    pj = lax.broadcasted_iota(jnp.int32, (c, PACK), 1) & (HEAD - 1)
    strict_m = earlier(pj, pt, True)
    incl_m = earlier(pj, pt, False)
    eye_m = pj == pt
    level_m = [((pt >> (l + 1)) == (pj >> (l + 1))) & ((pt >> l) != (pj >> l))
               for l in range(c.bit_length() - 1)]
    hshift = HEAD.bit_length() - 1
    br = lax.broadcasted_iota(jnp.int32, (PACK, PACK), 0) >> hshift
    bc = lax.broadcasted_iota(jnp.int32, (PACK, PACK), 1) >> hshift
    bmask = br == bc
    last = 0 if reverse else c - 1

    packs = range(npack)
    sls = [slice(q * PACK, (q + 1) * PACK) for q in packs]
    each = lambda f, *cols: [f(*args) for args in zip(*cols)]

    def cum_logdecay(sl):
        lw = lw_ref[:, sl]
        hi = lw.astype(BF16)
        lo = (lw - hi.astype(F32)).astype(BF16)
        return _dot(tri, hi) + _dot(tri, lo)

    cl = each(cum_logdecay, sls)
    tot = [x[last:last + 1, :] for x in cl]
    e_neg = [jnp.exp(-x) for x in cl]
    kd = [k_ref[:, sl] * (1.0 + (as_ref[:, sl] - 1.0) * ka_ref[:, sl]) for sl in sls]
    bq = [kk_ref[:, sl] * as_ref[:, sl] for sl in sls]
    lhs = [jnp.concatenate([-kk_ref[:, sl] * jnp.exp(x - lw_ref[:, sl]), r_ref[:, sl] * jnp.exp(x)],
                           axis=0).astype(BF16) for sl, x in zip(sls, cl)]
    mb = each(lambda l, b, e: _dot_nt(l, _block_diag(b * e, bmask)), lhs, bq, e_neg)
    mk = each(lambda l, k, e: _dot_nt(l, _block_diag(k * e, bmask)), lhs, kd, e_neg)
    a_s = [_dot_nt(l, s_ref[q].astype(BF16)) for q, l in zip(packs, lhs)]
    v_bd = [_block_diag(v_ref[:, sl], bmask) for sl in sls]
    w_in = each(lambda a, m, vb: a[:c] + _dot(jnp.where(strict_m, m[:c], 0.0).astype(BF16), vb),
                a_s, mk, v_bd)
    mab = [jnp.where(strict_m, m[:c], 0.0) for m in mb]
    tinv = [jnp.where(eye_m, 1.0, 0.0) + jnp.where(level_m[0], m, 0.0) for m in mab]
    for lm in level_m[1:]:
        dx = each(lambda t, m: _dot(t.astype(BF16), _block_diag(jnp.where(lm, m, 0.0), bmask)), tinv, mab)
        tinv = each(lambda t, x: t + _dot(x.astype(BF16), _block_diag(t, bmask)), tinv, dx)
    u = each(lambda t, w: _dot(t.astype(BF16), _block_diag(w, bmask)), tinv, w_in)
    for sl, a, b, k, x, vb in zip(sls, a_s, mb, mk, u, v_bd):
        y_ref[:, sl] = a[c:] + _dot(jnp.where(incl_m, b[c:], 0.0).astype(BF16), _block_diag(x, bmask)) \
            + _dot(jnp.where(incl_m, k[c:], 0.0).astype(BF16), vb)
    for q, sl, x, b, k, tt, cc in zip(packs, sls, u, bq, kd, tot, cl):
        e_end = jnp.exp(tt - cc)
        uv_t = jnp.transpose(jnp.concatenate([x, v_ref[:, sl]], axis=0)).astype(BF16)
        bk = jnp.concatenate([b * e_end, k * e_end], axis=0).astype(BF16)
        s_ref[q] = s_ref[q] * jnp.exp(tt) + jnp.where(bmask, _dot(uv_t, bk), 0.0)


def wkv_scan(r, k, v, kk, a_sig, lw, k_a, *, reverse, nbatch, seq, ctx_len, lane_block):
    t, d = r.shape
    c = SCAN_CHUNK
    lane_block = min(lane_block, d)
    assert seq % c == 0 and ctx_len % c == 0 and d % lane_block == 0 and lane_block % PACK == 0
    n_lat_c, n_ctx_c = seq // c, ctx_len // c
    ctx0 = nbatch * n_lat_c

    def blk(b, hg, ci):
        cc = (n_ctx_c - 1 - ci) if reverse else ci
        lc = ci - n_ctx_c
        lc = (n_lat_c - 1 - lc) if reverse else lc
        return (jnp.where(ci < n_ctx_c, ctx0 + b * n_ctx_c + cc, b * n_lat_c + lc), hg)

    row = pl.BlockSpec((c, lane_block), blk)
    npack = lane_block // PACK
    return pl.pallas_call(
        functools.partial(_scan_kernel, reverse=reverse, npack=npack),
        name="wkv_scan_rev" if reverse else "wkv_scan_fwd",
        out_shape=jax.ShapeDtypeStruct((t, d), F32),
        grid=(nbatch, d // lane_block, n_ctx_c + n_lat_c),
        in_specs=[row] * 6 + [pl.BlockSpec((1, lane_block), lambda b, hg, ci: (0, hg))],
        out_specs=row,
        scratch_shapes=[pltpu.VMEM((npack, PACK, PACK), F32)],
        compiler_params=_params("parallel", "parallel", "arbitrary"),
    )(r, k, v, kk, a_sig, lw, k_a)


def _rwkv_out_kernel(y0_ref, y1_ref, r_ref, k_ref, v_ref, as0_ref, as1_ref, g_ref,
                     ka_ref, rk_ref, lng_ref, lnb_ref, bd_ref, o_ref):
    bd = bd_ref[...]
    y = y0_ref[...] + y1_ref[...]
    inv = 1.0 / HEAD
    dy = y - _headsum(y, bd) * inv
    yn = dy * lax.rsqrt(_headsum(dy * dy, bd) * inv + GN_EPS) * lng_ref[...] + lnb_ref[...]
    k_bonus = k_ref[...] * (1.0 + (0.5 * (as0_ref[...] + as1_ref[...]) - 1.0) * ka_ref[...])
    bonus = _headsum(r_ref[...] * k_bonus * rk_ref[...], bd) * v_ref[...]
    o_ref[...] = ((yn + bonus) * g_ref[...]).astype(o_ref.dtype)


def rwkv_out(y0, y1, r, k, v, as0, as1, g, k_a, r_k, ln_g, ln_b, bd, *, tr):
    t, d = r.shape
    assert t % tr == 0
    row = pl.BlockSpec((tr, d), lambda i: (i, 0))
    full = lambda a: pl.BlockSpec(a.shape, lambda i: (0,) * a.ndim)
    return pl.pallas_call(
        _rwkv_out_kernel,
        name="rwkv_out",
        out_shape=jax.ShapeDtypeStruct((t, d), BF16),
        grid=(t // tr,),
        in_specs=[row] * 8 + [full(k_a), full(r_k), full(ln_g), full(ln_b), full(bd)],
        out_specs=row,
        compiler_params=_params("parallel"),
    )(y0, y1, r, k, v, as0, as1, g, k_a, r_k, ln_g, ln_b, bd)


def _pad_to(a, axis, size):
    pad = [(0, 0)] * a.ndim
    pad[axis] = (0, size - a.shape[axis])
    return jnp.pad(a, pad)


def _round_up(n, m):
    return -(-n // m) * m


def kernel(x, c, ctx, c_ctx, mod_w1, mod_w2, mod_b, norm_g, ffn_w1, ffn_w3, ffn_w2, ab_w_in, ab_w_out, sgu_ln_g, sgu_ln_b, sgu_ws, sgu_bs, rwkv_mu, rwkv_w_rkv, rwkv_w_o, rwkv_w0, rwkv_w1, rwkv_w2, rwkv_a0, rwkv_a1, rwkv_a2, rwkv_g1, rwkv_g2, rwkv_k_k, rwkv_k_a, rwkv_r_k, rwkv_lnx_g, rwkv_lnx_b):
    nbatch, seq, d = x.shape
    ctx_len = ctx.shape[1]
    depth = mod_w1.shape[0]
    n_lat, n_ctx = nbatch * seq, nbatch * ctx_len
    sw = sgu_ln_g.shape[1]
    fw = ab_w_in.shape[2] - 2 * sw
    gw = fw // MIX_GROUPS
    lora = rwkv_w1.shape[3]
    tr = 256
    tm = 512
    tm_ffn = next(t for t in (1536, 1024, 768, 512, 256) if (n_lat + n_ctx) % t == 0)
    common = dict(tr=tr, seq=seq, nbatch=nbatch)

    cond = _pad_to(jax.nn.silu(jnp.concatenate([c, c_ctx[None, :]], axis=0)), 0, 8)
    mods = []
    for layer in range(depth):
        hid = matmul([cond], [mod_w1[layer]], tm=8, tn=mod_w1.shape[2])
        m = matmul([hid], [mod_w2[layer]], tm=8, tn=2048) + mod_b[layer]
        mods.append(_pad_to(m[:nbatch + 1].reshape(nbatch + 1, 6, d), 1, 8))
    gains = norm_g.reshape(depth * 4, 1, d)
    gain = lambda layer, j: gains[layer * 4 + j:layer * 4 + j + 1]

    quarter = d // 4
    omega = POS_BASE ** (-jnp.arange(quarter, dtype=F32) / quarter)
    ang_r = jnp.arange(seq // GRID_W, dtype=F32)[:, None] * omega
    ang_c = jnp.arange(GRID_W, dtype=F32)[:, None] * omega
    pos = jnp.concatenate(
        [jnp.repeat(jnp.concatenate([jnp.sin(ang_r), jnp.cos(ang_r)], axis=-1), GRID_W, axis=0),
         jnp.tile(jnp.concatenate([jnp.sin(ang_c), jnp.cos(ang_c)], axis=-1), (seq // GRID_W, 1))], axis=-1)

    cc, sc = _dft_cos_sin(gw)
    cs_chan = jnp.concatenate([cc, sc], axis=1).astype(BF16)
    pos_dft = {}
    for length in (seq, ctx_len):
        cl_, sl_ = _dft_cos_sin(length)
        pos_dft[length] = (cl_.astype(BF16), (-sl_).astype(BF16))
    lane = jnp.arange(LANES)
    bd = (lane[:, None] // HEAD == lane[None, :] // HEAD).astype(BF16)

    dffp = _round_up(ffn_w1.shape[2], 1024)
    xs, h = embed_and_norm(x.reshape(n_lat, d), ctx.reshape(n_ctx, d), pos, gain(0, 0), mods[0],
                           h_dtype=BF16, **common)

    for layer in range(depth):
        i = layer // 2
        last = layer == depth - 1
        if layer % 2 == 0:
            p = matmul([h], [ab_w_in[i].astype(BF16)], tm=tm, tn=1024)
            u, v, yb = ab_mid(p, cs_chan, sgu_ln_g[i][None, :], sgu_ln_b[i][None, :], sgu_ws[i],
                              sgu_bs[i].T, fw=fw, sw=sw)
            ya = jnp.concatenate([
                dft_positions(u, v, *pos_dft[seq], nbatch=nbatch, length=seq, row0=0,
                              scale=1.0 / math.sqrt(seq * gw), tm=1024, tn=1024, tk=2048),
                dft_positions(u, v, *pos_dft[ctx_len], nbatch=nbatch, length=ctx_len, row0=n_lat,
                              scale=1.0 / math.sqrt(ctx_len * gw), tm=1024, tn=1024, tk=2048)], axis=0)
            w_out = ab_w_out[i].astype(BF16)
            o = matmul([ya, yb], [w_out[:fw], w_out[fw:]], tm=tm, tn=1024, out_dtype=BF16, name="mm_ab_out")
        else:
            mixes = shift_mix(h, _pad_to(rwkv_mu[i], 0, 8), tr=128, n_lat=n_lat, seq=seq, ctx_len=ctx_len)
            xr, xw, xk, xv, xa, xg = mixes
            w_rkv = rwkv_w_rkv[i].astype(BF16)
            r = matmul([xr], [w_rkv[0]], tm=tm, tn=1024)
            k = matmul([xk], [w_rkv[1]], tm=tm, tn=1024)
            v = matmul([xv], [w_rkv[2]], tm=tm, tn=1024)
            hw = matmul([xw], [jnp.concatenate([rwkv_w1[i, 0], rwkv_w1[i, 1]], axis=1).astype(BF16)],
                        tm=tm, tn=2 * lora, act="tanh")
            ha = matmul([xa], [jnp.concatenate([rwkv_a1[i, 0], rwkv_a1[i, 1]], axis=1).astype(BF16)],
                        tm=tm, tn=2 * lora)
            glp = _round_up(rwkv_g1.shape[2], LANES)
            hg = matmul([xg], [_pad_to(rwkv_g1[i], 1, glp).astype(BF16)], tm=tm, tn=glp,
                        act="sigmoid", out_dtype=BF16)
            g = matmul([hg], [_pad_to(rwkv_g2[i], 0, glp).astype(BF16)], tm=tm, tn=1024)
            k_a = rwkv_k_a[i][None, :]
            kk, lw0, lw1, as0, as1 = rwkv_prep(
                k, hw, ha, rwkv_w2[i].astype(BF16), rwkv_a2[i].astype(BF16), rwkv_w0[i], rwkv_a0[i],
                rwkv_k_k[i][None, :], bd, tr=128, lora=lora)
            scan = functools.partial(wkv_scan, r, k, v, kk, nbatch=nbatch, seq=seq, ctx_len=ctx_len,
                                     lane_block=2048)
            y0 = scan(as0, lw0, k_a, reverse=False)
            y1 = scan(as1, lw1, k_a, reverse=True)
            ob = rwkv_out(y0, y1, r, k, v, as0, as1, g, k_a, rwkv_r_k[i].reshape(1, d),
                          rwkv_lnx_g[i][None, :], rwkv_lnx_b[i][None, :], bd, tr=128)
            o = matmul([ob], [rwkv_w_o[i].astype(BF16)], tm=tm, tn=1024, out_dtype=BF16, name="mm_rwkv_out")

        xs, h = gated_residual(xs, o, gain(layer, 1), mods[layer], 2, gain_next=gain(layer, 2),
                               mods_next=mods[layer], h_dtype=BF16, **common)
        hid = swiglu_up(h, _pad_to(ffn_w1[layer], 1, dffp).astype(BF16),
                        _pad_to(ffn_w3[layer], 1, dffp).astype(BF16), tm=tm_ffn, tn=256)
        f = matmul([hid], [_pad_to(ffn_w2[layer], 0, dffp).astype(BF16)], tm=tm_ffn, tn=512,
                   tk=dffp // 4, out_dtype=BF16, name="mm_ffn_down")
        if last:
            xs = gated_residual(xs, f, gain(layer, 3), mods[layer], 5, rows=n_lat, **common)
        else:
            xs, h = gated_residual(xs, f, gain(layer, 3), mods[layer], 5, gain_next=gain(layer + 1, 0),
                                   mods_next=mods[layer + 1],
                                   h_dtype=F32 if (layer + 1) % 2 else BF16, **common)
    return xs.reshape(nbatch, seq, d)
```

```python
import functools
import math

import jax
import jax.numpy as jnp
from jax import lax
from jax.experimental import pallas as pl
from jax.experimental.pallas import tpu as pltpu

F32 = jnp.float32
BF16 = jnp.bfloat16

GRID_W = 64
SGU_CHUNK = 128
MIX_GROUPS = 4
HEAD = 64
RMS_EPS, LN_EPS, GN_EPS, L2_EPS = 1e-6, 1e-5, 64e-5, 1e-12
POS_BASE = 10000.0
DECAY_SCALE = math.exp(-0.5)

LANES = 128
MXU_DIM = 256
SCAN_CHUNK = 64
PACK = MXU_DIM
HEADS_PER_PACK = PACK // HEAD
VMEM_LIMIT_BYTES = 48 * 1024 * 1024


def _params(*sem):
    return pltpu.CompilerParams(dimension_semantics=sem, vmem_limit_bytes=VMEM_LIMIT_BYTES)


def _fit(n, t):
    t = min(t, n)
    while n % t:
        t -= LANES
    return t


def _dot(a, b):
    return jnp.dot(a, b, preferred_element_type=F32)


def _dot_nt(a, b):
    return lax.dot_general(a, b, (((1,), (1,)), ((), ())), preferred_element_type=F32)


def _act(x, act):
    if act == "tanh":
        return jnp.tanh(x)
    if act == "sigmoid":
        return jax.nn.sigmoid(x)
    assert act is None
    return x


def _gelu(x):
    return 0.5 * x * (1.0 + jnp.tanh(math.sqrt(2.0 / math.pi) * (x + 0.044715 * (x * x * x))))


def _rms(x, g):
    return x * lax.rsqrt(jnp.mean(x * x, axis=-1, keepdims=True) + RMS_EPS) * g


def _mm_kernel(*refs, npair, nk, kaxis, act, scale):
    o_ref = refs[2 * npair]

    def product():
        part = None
        for a_ref, b_ref in zip(refs[:npair], refs[npair:2 * npair]):
            d = _dot(a_ref[...].astype(BF16), b_ref[...].astype(BF16))
            part = d if part is None else part + d
        return part

    def finish(acc):
        if scale is not None:
            acc = acc * scale
        o_ref[...] = _act(acc, act).astype(o_ref.dtype)

    if nk == 1:
        finish(product())
        return
    acc_ref = refs[2 * npair + 1]
    k = pl.program_id(kaxis)

    @pl.when(k == 0)
    def _():
        acc_ref[...] = jnp.zeros_like(acc_ref)

    acc_ref[...] += product()

    @pl.when(k == nk - 1)
    def _():
        finish(acc_ref[...])


def matmul(a_list, b_list, *, tm, tn, tk=None, out_dtype=F32, act=None, scale=None, name="mm"):
    m, kdim = a_list[0].shape
    n = b_list[0].shape[1]
    tk = kdim if tk is None else tk
    tm, tn = min(tm, m), _fit(n, tn)
    assert m % tm == 0 and n % tn == 0 and kdim % tk == 0, (m, n, kdim, tm, tn, tk)
    nk = kdim // tk
    npair = len(a_list)
    kern = functools.partial(_mm_kernel, npair=npair, nk=nk, kaxis=2, act=act, scale=scale)
    return pl.pallas_call(
        kern,
        name=name,
        out_shape=jax.ShapeDtypeStruct((m, n), out_dtype),
        grid=(m // tm, n // tn, nk),
        in_specs=[pl.BlockSpec((tm, tk), lambda i, j, k: (i, k)) for _ in a_list]
        + [pl.BlockSpec((tk, tn), lambda i, j, k: (k, j)) for _ in b_list],
        out_specs=pl.BlockSpec((tm, tn), lambda i, j, k: (i, j)),
        scratch_shapes=[pltpu.VMEM((tm, tn), F32)] if nk > 1 else [],
        compiler_params=_params("parallel", "parallel", "arbitrary"),
    )(*a_list, *b_list)


def dft_positions(u, v, cmat, smat_neg, *, nbatch, length, row0, scale, tm, tn, tk):
    width = u.shape[1]
    tm, tk, tn = min(tm, length), min(tk, length), min(tn, width)
    assert length % tm == 0 and length % tk == 0 and width % tn == 0 and row0 % tk == 0
    nk, ni = length // tk, length // tm
    kern = functools.partial(_mm_kernel, npair=2, nk=nk, kaxis=3, act=None, scale=scale)
    a_spec = pl.BlockSpec((tm, tk), lambda b, i, j, k: (i, k))
    b_spec = pl.BlockSpec((tk, tn), lambda b, i, j, k: (row0 // tk + b * nk + k, j))
    return pl.pallas_call(
        kern,
        name="dft_positions",
        out_shape=jax.ShapeDtypeStruct((nbatch * length, width), BF16),
        grid=(nbatch, ni, width // tn, nk),
        in_specs=[a_spec, a_spec, b_spec, b_spec],
        out_specs=pl.BlockSpec((tm, tn), lambda b, i, j, k: (b * ni + i, j)),
        scratch_shapes=[pltpu.VMEM((tm, tn), F32)] if nk > 1 else [],
        compiler_params=_params("parallel", "parallel", "parallel", "arbitrary"),
    )(cmat, smat_neg, u, v)


def _swiglu_kernel(h_ref, w1_ref, w3_ref, o_ref):
    h = h_ref[...]
    a = _dot(h, w1_ref[...])
    b = _dot(h, w3_ref[...])
    o_ref[...] = (a * jax.nn.sigmoid(a) * b).astype(o_ref.dtype)


def swiglu_up(h, w1, w3, *, tm, tn):
    m, kdim = h.shape
    n = w1.shape[1]
    tm, tn = min(tm, m), min(tn, n)
    assert m % tm == 0 and n % tn == 0
    w_spec = pl.BlockSpec((kdim, tn), lambda i, j: (0, j))
    return pl.pallas_call(
        _swiglu_kernel,
        name="swiglu_up",
        out_shape=jax.ShapeDtypeStruct((m, n), BF16),
        grid=(m // tm, n // tn),
        in_specs=[pl.BlockSpec((tm, kdim), lambda i, j: (i, 0)), w_spec, w_spec],
        out_specs=pl.BlockSpec((tm, tn), lambda i, j: (i, j)),
        compiler_params=_params("parallel", "parallel"),
    )(h, w1, w3)


def _group_map(tr, seq, nbatch):
    return lambda i: (jnp.minimum((i * tr) // seq, nbatch), 0, 0)


def _embed_kernel(x_ref, ctx_ref, pos_ref, g_ref, mod_ref, xo_ref, h_ref, *, n_lat_tiles):
    is_lat = pl.program_id(0) < n_lat_tiles
    x0 = jnp.where(is_lat, x_ref[...] + pos_ref[...], ctx_ref[...])
    xo_ref[...] = x0
    h = _rms(x0, g_ref[...]) * (1.0 + mod_ref[1:2, :]) + mod_ref[0:1, :]
    h_ref[...] = h.astype(h_ref.dtype)


def embed_and_norm(x2, ctx2, pos, gain, mods, *, tr, seq, nbatch, h_dtype):
    n_lat, d = x2.shape
    n_ctx = ctx2.shape[0]
    t = n_lat + n_ctx
    assert n_lat % tr == 0 and n_ctx % tr == 0 and seq % tr == 0
    nl, npos = n_lat // tr, seq // tr
    row = lambda f: pl.BlockSpec((tr, d), f)
    return pl.pallas_call(
        functools.partial(_embed_kernel, n_lat_tiles=nl),
        name="embed_norm",
        out_shape=(jax.ShapeDtypeStruct((t, d), F32), jax.ShapeDtypeStruct((t, d), h_dtype)),
        grid=(t // tr,),
        in_specs=[row(lambda i: (jnp.minimum(i, nl - 1), 0)),
                  row(lambda i: (jnp.maximum(i - nl, 0), 0)),
                  row(lambda i: (i % npos, 0)),
                  pl.BlockSpec((None, 1, d), lambda i: (0, 0, 0)),
                  pl.BlockSpec((None, 8, d), _group_map(tr, seq, nbatch))],
        out_specs=(row(lambda i: (i, 0)), row(lambda i: (i, 0))),
        compiler_params=_params("parallel"),
    )(x2, ctx2, pos, gain, mods)


def _resid_kernel(*refs, gate_row, with_h):
    if with_h:
        x_ref, o_ref, go_ref, modc_ref, gn_ref, modn_ref, xo_ref, h_ref = refs
    else:
        x_ref, o_ref, go_ref, modc_ref, xo_ref = refs
    xn = x_ref[...] + modc_ref[gate_row:gate_row + 1, :] * _rms(o_ref[...].astype(F32), go_ref[...])
    xo_ref[...] = xn
    if with_h:
        shift_row = 3 if gate_row == 2 else 0
        h = _rms(xn, gn_ref[...]) * (1.0 + modn_ref[shift_row + 1:shift_row + 2, :]) \
            + modn_ref[shift_row:shift_row + 1, :]
        h_ref[...] = h.astype(h_ref.dtype)


def gated_residual(x, o, gain_o, mods_cur, gate_row, *, tr, seq, nbatch, rows=None,
                   gain_next=None, mods_next=None, h_dtype=None):
    t, d = x.shape
    rows = t if rows is None else rows
    assert rows % tr == 0
    with_h = gain_next is not None
    row = pl.BlockSpec((tr, d), lambda i: (i, 0))
    gain = pl.BlockSpec((None, 1, d), lambda i: (0, 0, 0))
    mod = pl.BlockSpec((None, 8, d), _group_map(tr, seq, nbatch))
    ins, in_specs = [x, o, gain_o, mods_cur], [row, row, gain, mod]
    out_shape, out_specs = [jax.ShapeDtypeStruct((rows, d), F32)], [row]
    if with_h:
        ins += [gain_next, mods_next]
        in_specs += [gain, mod]
        out_shape.append(jax.ShapeDtypeStruct((rows, d), h_dtype))
        out_specs.append(row)
    res = pl.pallas_call(
        functools.partial(_resid_kernel, gate_row=gate_row, with_h=with_h),
        name="gated_residual",
        out_shape=tuple(out_shape),
        grid=(rows // tr,),
        in_specs=in_specs,
        out_specs=tuple(out_specs),
        compiler_params=_params("parallel"),
    )(*ins)
    return res if with_h else res[0]


def _abmid_kernel(p_ref, cs_ref, lng_ref, lnb_ref, ws_ref, bs_ref, u_ref, v_ref, yb_ref, *, fw, sw):
    gw, sg = fw // MIX_GROUPS, sw // MIX_GROUPS
    cs = cs_ref[...]
    for g in range(MIX_GROUPS):
        uv = _dot(p_ref[:, g * gw:(g + 1) * gw].astype(BF16), cs)
        u_ref[:, g * gw:(g + 1) * gw] = uv[:, :gw].astype(BF16)
        v_ref[:, g * gw:(g + 1) * gw] = uv[:, gw:].astype(BF16)
    u = _gelu(p_ref[:, fw:fw + sw].astype(F32))
    gv = _gelu(p_ref[:, fw + sw:fw + 2 * sw].astype(F32))
    mu = jnp.mean(gv, axis=-1, keepdims=True)
    dv = gv - mu
    var = jnp.mean(dv * dv, axis=-1, keepdims=True)
    vn = (dv * lax.rsqrt(var + LN_EPS) * lng_ref[...] + lnb_ref[...]).astype(BF16)
    for g in range(MIX_GROUPS):
        s = _dot(ws_ref[g].astype(BF16), vn[:, g * sg:(g + 1) * sg]) + bs_ref[:, g:g + 1]
        yb_ref[:, g * sg:(g + 1) * sg] = (u[:, g * sg:(g + 1) * sg] * s).astype(BF16)


def ab_mid(p, cs, ln_g, ln_b, ws, bs_t, *, fw, sw):
    t = p.shape[0]
    assert t % SGU_CHUNK == 0
    full = lambda a: pl.BlockSpec(a.shape, lambda i: (0,) * a.ndim)
    out = lambda w: pl.BlockSpec((SGU_CHUNK, w), lambda i: (i, 0))
    return pl.pallas_call(
        functools.partial(_abmid_kernel, fw=fw, sw=sw),
        name="ab_mid",
        out_shape=(jax.ShapeDtypeStruct((t, fw), BF16), jax.ShapeDtypeStruct((t, fw), BF16),
                   jax.ShapeDtypeStruct((t, sw), BF16)),
        grid=(t // SGU_CHUNK,),
        in_specs=[pl.BlockSpec((SGU_CHUNK, p.shape[1]), lambda i: (i, 0)),
                  full(cs), full(ln_g), full(ln_b), full(ws), full(bs_t)],
        out_specs=(out(fw), out(fw), out(sw)),
        compiler_params=_params("parallel"),
    )(p, cs, ln_g, ln_b, ws, bs_t)


def _dft_cos_sin(n):
    blk = min(n, LANES)
    nb = n // blk
    k = jnp.arange(n, dtype=jnp.int32)[:, None]
    w = 2.0 * math.pi / n
    ang_t = w * ((k * jnp.arange(blk, dtype=jnp.int32)[None, :]) % n).astype(F32)
    ang_p = w * ((k * (jnp.arange(nb, dtype=jnp.int32) * blk)[None, :]) % n).astype(F32)
    ct, st = jnp.cos(ang_t)[:, None, :], jnp.sin(ang_t)[:, None, :]
    cp, sp = jnp.cos(ang_p)[:, :, None], jnp.sin(ang_p)[:, :, None]
    return (cp * ct - sp * st).reshape(n, n), (sp * ct + cp * st).reshape(n, n)


def _dftmat_kernel(ct_ref, st_ref, cp_ref, sp_ref, c_ref, s_ref):
    ct, st = ct_ref[...], st_ref[...]
    cp, sp = cp_ref[...], sp_ref[...]
    c_ref[...] = (cp * ct - sp * st).astype(c_ref.dtype)
    s_ref[...] = (-(sp * ct + cp * st)).astype(s_ref.dtype)


def dft_position_operators(n, *, tr=256, tc=2048):
    tr, tc = min(tr, n), min(tc, n)
    assert n % tr == 0 and n % tc == 0
    w = 2.0 * math.pi / n
    l = jnp.arange(n, dtype=jnp.int32)[None, :]
    ang_t = w * ((jnp.arange(tr, dtype=jnp.int32)[:, None] * l) % n).astype(F32)
    ang_p = w * (((jnp.arange(n // tr, dtype=jnp.int32) * tr)[:, None] * l) % n).astype(F32)
    table = pl.BlockSpec((tr, tc), lambda j, i: (0, j))
    phase = pl.BlockSpec((None, 1, tc), lambda j, i: (i, 0, j))
    out = pl.BlockSpec((tr, tc), lambda j, i: (i, j))
    return pl.pallas_call(
        _dftmat_kernel,
        name="dft_operators",
        out_shape=(jax.ShapeDtypeStruct((n, n), BF16), jax.ShapeDtypeStruct((n, n), BF16)),
        grid=(n // tc, n // tr),
        in_specs=[table, table, phase, phase],
        out_specs=(out, out),
        compiler_params=_params("parallel", "parallel"),
    )(jnp.cos(ang_t), jnp.sin(ang_t), jnp.cos(ang_p)[:, None, :], jnp.sin(ang_p)[:, None, :])


def _shiftmix_kernel(cur_ref, up_ref, dn_ref, mu_ref, *out_refs, tr, n_lat_tiles, lat_tiles_per_seq,
                     ctx_tiles_per_seq, d):
    i = pl.program_id(0)
    rows = lax.broadcasted_iota(jnp.int32, (tr, 1), 0)

    def emit(lo, hi, shifted):
        h = cur_ref[:, lo:hi]
        xx = shifted - h
        for m, o_ref in enumerate(out_refs):
            o_ref[:, lo:hi] = (h + xx * mu_ref[m:m + 1, lo:hi]).astype(o_ref.dtype)

    @pl.when(i < n_lat_tiles)
    def _():
        q = d // 4
        col = rows & (GRID_W - 1)
        ib = i % lat_tiles_per_seq
        up_ok = (ib > 0).astype(F32)
        dn_ok = (ib < lat_tiles_per_seq - 1).astype(F32)
        emit(0, q, jnp.where(col > 0, pltpu.roll(cur_ref[:, 0:q], 1, 0), 0.0))
        emit(q, 2 * q, jnp.where(col < GRID_W - 1, pltpu.roll(cur_ref[:, q:2 * q], tr - 1, 0), 0.0))
        emit(2 * q, 3 * q, jnp.concatenate(
            [up_ref[:, 2 * q:3 * q] * up_ok, cur_ref[0:tr - GRID_W, 2 * q:3 * q]], axis=0))
        emit(3 * q, d, jnp.concatenate(
            [cur_ref[GRID_W:tr, 3 * q:d], dn_ref[:, 3 * q:d] * dn_ok], axis=0))

    @pl.when(i >= n_lat_tiles)
    def _():
        hf = d // 2
        jb = (i - n_lat_tiles) % ctx_tiles_per_seq
        up_ok = (jb > 0).astype(F32)
        dn_ok = (jb < ctx_tiles_per_seq - 1).astype(F32)
        prev = jnp.where(rows == 0, up_ref[GRID_W - 1:GRID_W, 0:hf] * up_ok,
                         pltpu.roll(cur_ref[:, 0:hf], 1, 0))
        nxt = jnp.where(rows == tr - 1, dn_ref[0:1, hf:d] * dn_ok,
                        pltpu.roll(cur_ref[:, hf:d], tr - 1, 0))
        emit(0, hf, prev)
        emit(hf, d, nxt)


def shift_mix(h, mu8, *, tr, n_lat, seq, ctx_len):
    t, d = h.shape
    assert t % tr == 0 and tr % GRID_W == 0 and seq % tr == 0 and ctx_len % tr == 0 and tr > GRID_W
    per = tr // GRID_W
    nhalo = t // GRID_W
    halo = lambda f: pl.BlockSpec((GRID_W, d), f)
    row = pl.BlockSpec((tr, d), lambda i: (i, 0))
    kern = functools.partial(_shiftmix_kernel, tr=tr, n_lat_tiles=n_lat // tr,
                             lat_tiles_per_seq=seq // tr, ctx_tiles_per_seq=ctx_len // tr, d=d)
    return pl.pallas_call(
        kern,
        name="shift_mix",
        out_shape=tuple(jax.ShapeDtypeStruct((t, d), BF16) for _ in range(6)),
        grid=(t // tr,),
        in_specs=[row,
                  halo(lambda i: (jnp.maximum(i * per - 1, 0), 0)),
                  halo(lambda i: (jnp.minimum((i + 1) * per, nhalo - 1), 0)),
                  pl.BlockSpec((8, d), lambda i: (0, 0))],
        out_specs=tuple(row for _ in range(6)),
        compiler_params=_params("parallel"),
    )(h, h, h, mu8)


def _headsum(x, bd):
    r, d = x.shape
    nt = d // LANES
    xs = jnp.concatenate([x[:, t * LANES:(t + 1) * LANES] for t in range(nt)], axis=0)
    hi = xs.astype(BF16)
    lo = (xs - hi.astype(F32)).astype(BF16)
    s = _dot(hi, bd) + _dot(lo, bd)
    return jnp.concatenate([s[t * r:(t + 1) * r] for t in range(nt)], axis=1)


def _prep_kernel(k_ref, hw_ref, ha_ref, w2_ref, a2_ref, w0_ref, a0_ref, kkp_ref, bd_ref,
                 kk_ref, lw0_ref, lw1_ref, as0_ref, as1_ref, *, lora):
    for dr, (lw_ref, as_ref) in enumerate(((lw0_ref, as0_ref), (lw1_ref, as1_ref))):
        z = _dot(hw_ref[:, dr * lora:(dr + 1) * lora].astype(BF16), w2_ref[dr]) + w0_ref[dr:dr + 1, :]
        lw_ref[...] = -DECAY_SCALE * jax.nn.sigmoid(z)
        za = _dot(ha_ref[:, dr * lora:(dr + 1) * lora].astype(BF16), a2_ref[dr]) + a0_ref[dr:dr + 1, :]
        as_ref[...] = jax.nn.sigmoid(za).astype(as_ref.dtype)
    kq = k_ref[...] * kkp_ref[...]
    kk_ref[...] = kq * lax.rsqrt(_headsum(kq * kq, bd_ref[...]) + L2_EPS)


def rwkv_prep(k, hw, ha, w2, a2, w0, a0, k_k, bd, *, tr, lora):
    t, d = k.shape
    assert t % tr == 0
    row = pl.BlockSpec((tr, d), lambda i: (i, 0))
    lrow = pl.BlockSpec((tr, 2 * lora), lambda i: (i, 0))
    full = lambda a: pl.BlockSpec(a.shape, lambda i: (0,) * a.ndim)
    return pl.pallas_call(
        functools.partial(_prep_kernel, lora=lora),
        name="rwkv_prep",
        out_shape=tuple(jax.ShapeDtypeStruct((t, d), dt) for dt in (F32, F32, F32, BF16, BF16)),
        grid=(t // tr,),
        in_specs=[row, lrow, lrow, full(w2), full(a2), full(w0), full(a0), full(k_k), full(bd)],
        out_specs=tuple(row for _ in range(5)),
        compiler_params=_params("parallel"),
    )(k, hw, ha, w2, a2, w0, a0, k_k, bd)


def _block_diag(x, mask):
    tiled = jnp.concatenate([x] * HEADS_PER_PACK, axis=0)
    return jnp.where(mask, tiled, 0.0).astype(BF16)


def _scan_kernel(r_ref, k_ref, v_ref, kk_ref, as_ref, lw_ref, ka_ref, y_ref, s_ref, *, reverse, npack,
                 nsub):
    c = SCAN_CHUNK

    @pl.when(pl.program_id(2) == 0)
    def _():
        s_ref[...] = jnp.zeros_like(s_ref)

    def earlier(a, b, strict):
        if reverse:
            return (a > b) if strict else (a >= b)
        return (a < b) if strict else (a <= b)

    ti = lax.broadcasted_iota(jnp.int32, (c, c), 0)
    tj = lax.broadcasted_iota(jnp.int32, (c, c), 1)
    tri = jnp.where(earlier(tj, ti, False), 1.0, 0.0).astype(BF16)
    pt = lax.broadcasted_iota(jnp.int32, (c, PACK), 0)
    pj = lax.broadcasted_iota(jnp.int32, (c, PACK), 1) & (HEAD - 1)
    strict_m = earlier(pj, pt, True)
    incl_m = earlier(pj, pt, False)
    eye_m = pj == pt
    level_m = [((pt >> (l + 1)) == (pj >> (l + 1))) & ((pt >> l) != (pj >> l))
               for l in range(c.bit_length() - 1)]
    hshift = HEAD.bit_length() - 1
    br = lax.broadcasted_iota(jnp.int32, (PACK, PACK), 0) >> hshift
    bc = lax.broadcasted_iota(jnp.int32, (PACK, PACK), 1) >> hshift
    bmask = br == bc
    last = 0 if reverse else c - 1

    packs = range(npack)
    sls = [slice(q * PACK, (q + 1) * PACK) for q in packs]
    each = lambda f, *cols: [f(*args) for args in zip(*cols)]

    def chunk(rows):
        def cum_logdecay(sl):
            lw = lw_ref[rows, sl]
            hi = lw.astype(BF16)
            lo = (lw - hi.astype(F32)).astype(BF16)
            return _dot(tri, hi) + _dot(tri, lo)

        cl = each(cum_logdecay, sls)
        tot = [x[last:last + 1, :] for x in cl]
        e_neg = [jnp.exp(-x) for x in cl]
        asg = [as_ref[rows, sl].astype(F32) for sl in sls]
        kd = [k_ref[rows, sl] * (1.0 + (a - 1.0) * ka_ref[:, sl]) for sl, a in zip(sls, asg)]
        bq = [kk_ref[rows, sl] * a for sl, a in zip(sls, asg)]
        lhs = [jnp.concatenate([-kk_ref[rows, sl] * jnp.exp(x - lw_ref[rows, sl]),
                                r_ref[rows, sl] * jnp.exp(x)], axis=0).astype(BF16)
               for sl, x in zip(sls, cl)]
        mb = each(lambda l, b, e: _dot_nt(l, _block_diag(b * e, bmask)), lhs, bq, e_neg)
        mk = each(lambda l, k, e: _dot_nt(l, _block_diag(k * e, bmask)), lhs, kd, e_neg)
        a_s = [_dot_nt(l, s_ref[q].astype(BF16)) for q, l in zip(packs, lhs)]
        v_bd = [_block_diag(v_ref[rows, sl], bmask) for sl in sls]
        w_in = each(lambda a, m, vb: a[:c] + _dot(jnp.where(strict_m, m[:c], 0.0).astype(BF16), vb),
                    a_s, mk, v_bd)
        mab = [jnp.where(strict_m, m[:c], 0.0) for m in mb]
        tinv = [jnp.where(eye_m, 1.0, 0.0) + jnp.where(level_m[0], m, 0.0) for m in mab]
        for lm in level_m[1:]:
            dx = each(lambda t, m: _dot(t.astype(BF16), _block_diag(jnp.where(lm, m, 0.0), bmask)),
                      tinv, mab)
            tinv = each(lambda t, x: t + _dot(x.astype(BF16), _block_diag(t, bmask)), tinv, dx)
        u = each(lambda t, w: _dot(t.astype(BF16), _block_diag(w, bmask)), tinv, w_in)
        for sl, a, b, k, x, vb in zip(sls, a_s, mb, mk, u, v_bd):
            y = a[c:] + _dot(jnp.where(incl_m, b[c:], 0.0).astype(BF16), _block_diag(x, bmask)) \
                + _dot(jnp.where(incl_m, k[c:], 0.0).astype(BF16), vb)
            y_ref[rows, sl] = y.astype(y_ref.dtype)
        for q, sl, x, b, k, tt, cc in zip(packs, sls, u, bq, kd, tot, cl):
            e_end = jnp.exp(tt - cc)
            uv_t = jnp.transpose(jnp.concatenate([x, v_ref[rows, sl]], axis=0)).astype(BF16)
            bk = jnp.concatenate([b * e_end, k * e_end], axis=0).astype(BF16)
            s_ref[q] = s_ref[q] * jnp.exp(tt) + jnp.where(bmask, _dot(uv_t, bk), 0.0)

    subs = range(nsub - 1, -1, -1) if reverse else range(nsub)
    for sub in subs:
        chunk(slice(sub * c, (sub + 1) * c))


def wkv_scan(r, k, v, kk, a_sig, lw, k_a, *, reverse, nbatch, seq, ctx_len, lane_block,
             chunks_per_step=2):
    t, d = r.shape
    c = SCAN_CHUNK * chunks_per_step
    lane_block = min(lane_block, d)
    assert seq % c == 0 and ctx_len % c == 0 and d % lane_block == 0 and lane_block % PACK == 0
    n_lat_c, n_ctx_c = seq // c, ctx_len // c
    ctx0 = nbatch * n_lat_c

    def blk(b, hg, ci):
        cc = (n_ctx_c - 1 - ci) if reverse else ci
        lc = ci - n_ctx_c
        lc = (n_lat_c - 1 - lc) if reverse else lc
        return (jnp.where(ci < n_ctx_c, ctx0 + b * n_ctx_c + cc, b * n_lat_c + lc), hg)

    row = pl.BlockSpec((c, lane_block), blk)
    npack = lane_block // PACK
    return pl.pallas_call(
        functools.partial(_scan_kernel, reverse=reverse, npack=npack, nsub=chunks_per_step),
        name="wkv_scan_rev" if reverse else "wkv_scan_fwd",
        out_shape=jax.ShapeDtypeStruct((t, d), BF16),
        grid=(nbatch, d // lane_block, n_ctx_c + n_lat_c),
        in_specs=[row] * 6 + [pl.BlockSpec((1, lane_block), lambda b, hg, ci: (0, hg))],
        out_specs=row,
        scratch_shapes=[pltpu.VMEM((npack, PACK, PACK), F32)],
        compiler_params=_params("parallel", "parallel", "arbitrary"),
    )(r, k, v, kk, a_sig, lw, k_a)


def _rwkv_out_kernel(y0_ref, y1_ref, r_ref, k_ref, v_ref, as0_ref, as1_ref, g_ref,
                     ka_ref, rk_ref, lng_ref, lnb_ref, bd_ref, o_ref):
    bd = bd_ref[...]
    y = y0_ref[...].astype(F32) + y1_ref[...].astype(F32)
    inv = 1.0 / HEAD
    dy = y - _headsum(y, bd) * inv
    yn = dy * lax.rsqrt(_headsum(dy * dy, bd) * inv + GN_EPS) * lng_ref[...] + lnb_ref[...]
    a_mean = 0.5 * (as0_ref[...].astype(F32) + as1_ref[...].astype(F32))
    k_bonus = k_ref[...] * (1.0 + (a_mean - 1.0) * ka_ref[...])
    bonus = _headsum(r_ref[...] * k_bonus * rk_ref[...], bd) * v_ref[...]
    o_ref[...] = ((yn + bonus) * g_ref[...].astype(F32)).astype(o_ref.dtype)


def rwkv_out(y0, y1, r, k, v, as0, as1, g, k_a, r_k, ln_g, ln_b, bd, *, tr):
    t, d = r.shape
    assert t % tr == 0
    row = pl.BlockSpec((tr, d), lambda i: (i, 0))
    full = lambda a: pl.BlockSpec(a.shape, lambda i: (0,) * a.ndim)
    return pl.pallas_call(
        _rwkv_out_kernel,
        name="rwkv_out",
        out_shape=jax.ShapeDtypeStruct((t, d), BF16),
        grid=(t // tr,),
        in_specs=[row] * 8 + [full(k_a), full(r_k), full(ln_g), full(ln_b), full(bd)],
        out_specs=row,
        compiler_params=_params("parallel"),
    )(y0, y1, r, k, v, as0, as1, g, k_a, r_k, ln_g, ln_b, bd)


def _pad_to(a, axis, size):
    pad = [(0, 0)] * a.ndim
    pad[axis] = (0, size - a.shape[axis])
    return jnp.pad(a, pad)


def _round_up(n, m):
    return -(-n // m) * m


def kernel(x, c, ctx, c_ctx, mod_w1, mod_w2, mod_b, norm_g, ffn_w1, ffn_w3, ffn_w2, ab_w_in, ab_w_out, sgu_ln_g, sgu_ln_b, sgu_ws, sgu_bs, rwkv_mu, rwkv_w_rkv, rwkv_w_o, rwkv_w0, rwkv_w1, rwkv_w2, rwkv_a0, rwkv_a1, rwkv_a2, rwkv_g1, rwkv_g2, rwkv_k_k, rwkv_k_a, rwkv_r_k, rwkv_lnx_g, rwkv_lnx_b):
    nbatch, seq, d = x.shape
    ctx_len = ctx.shape[1]
    depth = mod_w1.shape[0]
    n_lat, n_ctx = nbatch * seq, nbatch * ctx_len
    sw = sgu_ln_g.shape[1]
    fw = ab_w_in.shape[2] - 2 * sw
    gw = fw // MIX_GROUPS
    lora = rwkv_w1.shape[3]
    tr = 256
    tm = 512
    tm_ffn = next(t for t in (1536, 1024, 768, 512, 256) if (n_lat + n_ctx) % t == 0)
    common = dict(tr=tr, seq=seq, nbatch=nbatch)

    cond = _pad_to(jax.nn.silu(jnp.concatenate([c, c_ctx[None, :]], axis=0)), 0, 8)
    mods = []
    for layer in range(depth):
        hid = matmul([cond], [mod_w1[layer]], tm=8, tn=mod_w1.shape[2])
        m = matmul([hid], [mod_w2[layer]], tm=8, tn=2048) + mod_b[layer]
        mods.append(_pad_to(m[:nbatch + 1].reshape(nbatch + 1, 6, d), 1, 8))
    gains = norm_g.reshape(depth * 4, 1, d)
    gain = lambda layer, j: gains[layer * 4 + j:layer * 4 + j + 1]

    quarter = d // 4
    omega = POS_BASE ** (-jnp.arange(quarter, dtype=F32) / quarter)
    ang_r = jnp.arange(seq // GRID_W, dtype=F32)[:, None] * omega
    ang_c = jnp.arange(GRID_W, dtype=F32)[:, None] * omega
    pos = jnp.concatenate(
        [jnp.repeat(jnp.concatenate([jnp.sin(ang_r), jnp.cos(ang_r)], axis=-1), GRID_W, axis=0),
         jnp.tile(jnp.concatenate([jnp.sin(ang_c), jnp.cos(ang_c)], axis=-1), (seq // GRID_W, 1))], axis=-1)

    cc, sc = _dft_cos_sin(gw)
    cs_chan = jnp.concatenate([cc, sc], axis=1).astype(BF16)
    pos_dft = {length: dft_position_operators(length) for length in (seq, ctx_len)}
    lane = jnp.arange(LANES)
    bd = (lane[:, None] // HEAD == lane[None, :] // HEAD).astype(BF16)

    dff = ffn_w1.shape[2]
    xs, h = embed_and_norm(x.reshape(n_lat, d), ctx.reshape(n_ctx, d), pos, gain(0, 0), mods[0],
                           h_dtype=BF16, **common)

    for layer in range(depth):
        i = layer // 2
        last = layer == depth - 1
        if layer % 2 == 0:
            p = matmul([h], [ab_w_in[i].astype(BF16)], tm=tm, tn=1024, out_dtype=BF16, name="mm_ab_in")
            u, v, yb = ab_mid(p, cs_chan, sgu_ln_g[i][None, :], sgu_ln_b[i][None, :], sgu_ws[i],
                              sgu_bs[i].T, fw=fw, sw=sw)
            ya = jnp.concatenate([
                dft_positions(u, v, *pos_dft[seq], nbatch=nbatch, length=seq, row0=0,
                              scale=1.0 / math.sqrt(seq * gw), tm=1024, tn=1024, tk=2048),
                dft_positions(u, v, *pos_dft[ctx_len], nbatch=nbatch, length=ctx_len, row0=n_lat,
                              scale=1.0 / math.sqrt(ctx_len * gw), tm=1024, tn=1024, tk=2048)], axis=0)
            w_out = ab_w_out[i].astype(BF16)
            o = matmul([ya, yb], [w_out[:fw], w_out[fw:]], tm=tm, tn=1024, out_dtype=BF16, name="mm_ab_out")
        else:
            mixes = shift_mix(h, _pad_to(rwkv_mu[i], 0, 8), tr=128, n_lat=n_lat, seq=seq, ctx_len=ctx_len)
            xr, xw, xk, xv, xa, xg = mixes
            w_rkv = rwkv_w_rkv[i].astype(BF16)
            r = matmul([xr], [w_rkv[0]], tm=tm, tn=1024)
            k = matmul([xk], [w_rkv[1]], tm=tm, tn=1024)
            v = matmul([xv], [w_rkv[2]], tm=tm, tn=1024)
            hw = matmul([xw], [jnp.concatenate([rwkv_w1[i, 0], rwkv_w1[i, 1]], axis=1).astype(BF16)],
                        tm=tm, tn=2 * lora, act="tanh")
            ha = matmul([xa], [jnp.concatenate([rwkv_a1[i, 0], rwkv_a1[i, 1]], axis=1).astype(BF16)],
                        tm=tm, tn=2 * lora)
            glp = _round_up(rwkv_g1.shape[2], LANES)
            hg = matmul([xg], [_pad_to(rwkv_g1[i], 1, glp).astype(BF16)], tm=tm, tn=glp,
                        act="sigmoid", out_dtype=BF16)
            g = matmul([hg], [_pad_to(rwkv_g2[i], 0, glp).astype(BF16)], tm=tm, tn=1024, out_dtype=BF16)
            k_a = rwkv_k_a[i][None, :]
            kk, lw0, lw1, as0, as1 = rwkv_prep(
                k, hw, ha, rwkv_w2[i].astype(BF16), rwkv_a2[i].astype(BF16), rwkv_w0[i], rwkv_a0[i],
                rwkv_k_k[i][None, :], bd, tr=128, lora=lora)
            scan = functools.partial(wkv_scan, r, k, v, kk, nbatch=nbatch, seq=seq, ctx_len=ctx_len,
                                     lane_block=2048)
            y0 = scan(as0, lw0, k_a, reverse=False)
            y1 = scan(as1, lw1, k_a, reverse=True)
            ob = rwkv_out(y0, y1, r, k, v, as0, as1, g, k_a, rwkv_r_k[i].reshape(1, d),
                          rwkv_lnx_g[i][None, :], rwkv_lnx_b[i][None, :], bd, tr=128)
            o = matmul([ob], [rwkv_w_o[i].astype(BF16)], tm=tm, tn=1024, out_dtype=BF16, name="mm_rwkv_out")

        xs, h = gated_residual(xs, o, gain(layer, 1), mods[layer], 2, gain_next=gain(layer, 2),
                               mods_next=mods[layer], h_dtype=BF16, **common)
        hid = swiglu_up(h, ffn_w1[layer].astype(BF16), ffn_w3[layer].astype(BF16), tm=tm_ffn, tn=256)
        f = matmul([hid], [ffn_w2[layer].astype(BF16)], tm=tm_ffn // 2, tn=512, tk=dff // 2,
                   out_dtype=BF16, name="mm_ffn_down")
        if last:
            xs = gated_residual(xs, f, gain(layer, 3), mods[layer], 5, rows=n_lat, **common)
        else:
            xs, h = gated_residual(xs, f, gain(layer, 3), mods[layer], 5, gain_next=gain(layer + 1, 0),
                                   mods_next=mods[layer + 1],
                                   h_dtype=F32 if (layer + 1) % 2 else BF16, **common)
    return xs.reshape(nbatch, seq, d)
```

```python
import functools
import math

import jax
import jax.numpy as jnp
from jax import lax
from jax.experimental import pallas as pl
from jax.experimental.pallas import tpu as pltpu

F32 = jnp.float32
BF16 = jnp.bfloat16

GRID_W = 64
SGU_CHUNK = 128
MIX_GROUPS = 4
HEAD = 64
RMS_EPS, LN_EPS, GN_EPS, L2_EPS = 1e-6, 1e-5, 64e-5, 1e-12
POS_BASE = 10000.0
DECAY_SCALE = math.exp(-0.5)

LANES = 128
MXU_DIM = 256
SCAN_CHUNK = 64
PACK = MXU_DIM
HEADS_PER_PACK = PACK // HEAD
VMEM_LIMIT_BYTES = 48 * 1024 * 1024


def _params(*sem):
    return pltpu.CompilerParams(dimension_semantics=sem, vmem_limit_bytes=VMEM_LIMIT_BYTES)


def _fit(n, t):
    t = min(t, n)
    while n % t:
        t -= LANES
    return t


def _dot(a, b):
    return jnp.dot(a, b, preferred_element_type=F32)


def _dot_nt(a, b):
    return lax.dot_general(a, b, (((1,), (1,)), ((), ())), preferred_element_type=F32)


def _act(x, act):
    if act == "tanh":
        return jnp.tanh(x)
    if act == "sigmoid":
        return jax.nn.sigmoid(x)
    assert act is None
    return x


def _gelu(x):
    return 0.5 * x * (1.0 + jnp.tanh(math.sqrt(2.0 / math.pi) * (x + 0.044715 * (x * x * x))))


def _rms(x, g):
    return x * lax.rsqrt(jnp.mean(x * x, axis=-1, keepdims=True) + RMS_EPS) * g


def _mm_kernel(*refs, npair, nk, kaxis, act, scale):
    o_ref = refs[2 * npair]

    def product():
        part = None
        for a_ref, b_ref in zip(refs[:npair], refs[npair:2 * npair]):
            d = _dot(a_ref[...].astype(BF16), b_ref[...].astype(BF16))
            part = d if part is None else part + d
        return part

    def finish(acc):
        if scale is not None:
            acc = acc * scale
        o_ref[...] = _act(acc, act).astype(o_ref.dtype)

    if nk == 1:
        finish(product())
        return
    acc_ref = refs[2 * npair + 1]
    k = pl.program_id(kaxis)

    @pl.when(k == 0)
    def _():
        acc_ref[...] = jnp.zeros_like(acc_ref)

    acc_ref[...] += product()

    @pl.when(k == nk - 1)
    def _():
        finish(acc_ref[...])


def _weight(b):
    if isinstance(b, tuple):
        return b[0], tuple(b[1]), (b[2] if len(b) > 2 else 0)
    return b, (), 0


def _weight_spec(b, tk, tn, kj):
    arr, lead, k0 = _weight(b)
    assert arr.ndim == len(lead) + 2
    return pl.BlockSpec((None,) * len(lead) + (tk, tn), lambda *g: lead + (kj(*g)[0] + k0, kj(*g)[1]))


def matmul(a_list, b_list, *, tm, tn, tk=None, out_dtype=F32, act=None, scale=None, name="mm"):
    m, kdim = a_list[0].shape
    n = _weight(b_list[0])[0].shape[-1]
    tk = kdim if tk is None else tk
    tm, tn = min(tm, m), _fit(n, tn)
    assert m % tm == 0 and n % tn == 0 and kdim % tk == 0, (m, n, kdim, tm, tn, tk)
    nk = kdim // tk
    npair = len(a_list)
    kern = functools.partial(_mm_kernel, npair=npair, nk=nk, kaxis=2, act=act, scale=scale)
    return pl.pallas_call(
        kern,
        name=name,
        out_shape=jax.ShapeDtypeStruct((m, n), out_dtype),
        grid=(m // tm, n // tn, nk),
        in_specs=[pl.BlockSpec((tm, tk), lambda i, j, k: (i, k)) for _ in a_list]
        + [_weight_spec(b, tk, tn, lambda i, j, k: (k, j)) for b in b_list],
        out_specs=pl.BlockSpec((tm, tn), lambda i, j, k: (i, j)),
        scratch_shapes=[pltpu.VMEM((tm, tn), F32)] if nk > 1 else [],
        compiler_params=_params("parallel", "parallel", "arbitrary"),
    )(*a_list, *[_weight(b)[0] for b in b_list])


def dft_positions(u, v, cmat, smat_neg, *, nbatch, length, row0, scale, tm, tn, tk):
    width = u.shape[1]
    tm, tk, tn = min(tm, length), min(tk, length), min(tn, width)
    assert length % tm == 0 and length % tk == 0 and width % tn == 0 and row0 % tk == 0
    nk, ni = length // tk, length // tm
    kern = functools.partial(_mm_kernel, npair=2, nk=nk, kaxis=3, act=None, scale=scale)
    a_spec = pl.BlockSpec((tm, tk), lambda b, i, j, k: (i, k))
    b_spec = pl.BlockSpec((tk, tn), lambda b, i, j, k: (row0 // tk + b * nk + k, j))
    return pl.pallas_call(
        kern,
        name="dft_positions",
        out_shape=jax.ShapeDtypeStruct((nbatch * length, width), F32),
        grid=(nbatch, ni, width // tn, nk),
        in_specs=[a_spec, a_spec, b_spec, b_spec],
        out_specs=pl.BlockSpec((tm, tn), lambda b, i, j, k: (b * ni + i, j)),
        scratch_shapes=[pltpu.VMEM((tm, tn), F32)] if nk > 1 else [],
        compiler_params=_params("parallel", "parallel", "parallel", "arbitrary"),
    )(cmat, smat_neg, u, v)


def _dft_stage2_kernel(gr_ref, gi_ref, zr_ref, zi_ref, o_ref, *, scale):
    y = (_dot(gr_ref[...], zr_ref[...]) + _dot(gi_ref[...], zi_ref[...])) * scale
    o_ref[...] = y.reshape(o_ref.shape).astype(o_ref.dtype)


def dft_positions_two_stage(u, v, *, batch, length, scale):
    width = u.shape[1]
    n2 = LANES
    n1 = length // n2
    sub = 8
    assert length % n2 == 0 and n1 % sub == 0 and u.shape[0] >= (batch + 1) * length
    c2, s2 = _dft_cos_sin(n2)
    a_u = jnp.concatenate([c2, -s2], axis=0).astype(BF16)
    a_v = jnp.concatenate([-s2, -c2], axis=0).astype(BF16)
    nb = u.shape[0] // length
    view = lambda a: a[:nb * length].reshape(nb, n2, n1 * width)
    z = matmul([a_u, a_v], [(view(u), (batch,)), (view(v), (batch,))], tm=2 * n2, tn=8192,
               out_dtype=BF16, name="dft_stage1")
    z = z.reshape(2, n2 * n1, width)
    t = jnp.arange(n2 // sub, dtype=jnp.int32)[:, None, None, None]
    k1 = jnp.arange(n1, dtype=jnp.int32)[None, :, None, None]
    j = jnp.arange(sub, dtype=jnp.int32)[None, None, :, None]
    l1 = jnp.arange(n1, dtype=jnp.int32)[None, None, None, :]
    ang = (2.0 * math.pi / length) * ((l1 * (n2 * k1 + sub * t + j)) % length).astype(F32)
    eye = jnp.eye(sub, dtype=F32)[None, None, :, :, None]
    op = lambda f: (f(ang)[:, :, :, None, :] * eye).reshape(n2 // sub, sub * n1, sub * n1).astype(BF16)
    g_spec = pl.BlockSpec((None, sub * n1, sub * n1), lambda i: (i, 0, 0))
    z_spec = lambda part: pl.BlockSpec((None, sub * n1, width), lambda i: (part, i, 0))
    out = pl.pallas_call(
        functools.partial(_dft_stage2_kernel, scale=scale),
        name="dft_stage2",
        out_shape=jax.ShapeDtypeStruct((n1, n2, width), F32),
        grid=(n2 // sub,),
        in_specs=[g_spec, g_spec, z_spec(0), z_spec(1)],
        out_specs=pl.BlockSpec((n1, sub, width), lambda i: (0, i, 0)),
        compiler_params=_params("parallel"),
    )(op(jnp.cos), op(jnp.sin), z, z)
    return out.reshape(length, width)


def _swiglu_kernel(h_ref, w1_ref, w3_ref, o_ref):
    h = h_ref[...]
    a = _dot(h, w1_ref[...])
    b = _dot(h, w3_ref[...])
    o_ref[...] = (a * jax.nn.sigmoid(a) * b).astype(o_ref.dtype)


def swiglu_up(h, w1, w3, *, tm, tn):
    m, kdim = h.shape
    n = _weight(w1)[0].shape[-1]
    tm, tn = min(tm, m), min(tn, n)
    assert m % tm == 0 and n % tn == 0
    w_spec = lambda w: _weight_spec(w, kdim, tn, lambda i, j: (0, j))
    return pl.pallas_call(
        _swiglu_kernel,
        name="swiglu_up",
        out_shape=jax.ShapeDtypeStruct((m, n), BF16),
        grid=(m // tm, n // tn),
        in_specs=[pl.BlockSpec((tm, kdim), lambda i, j: (i, 0)), w_spec(w1), w_spec(w3)],
        out_specs=pl.BlockSpec((tm, tn), lambda i, j: (i, j)),
        compiler_params=_params("parallel", "parallel"),
    )(h, _weight(w1)[0], _weight(w3)[0])


def _group_map(tr, seq, nbatch):
    return lambda i: (jnp.minimum((i * tr) // seq, nbatch), 0, 0)


def _embed_kernel(x_ref, ctx_ref, pos_ref, g_ref, mod_ref, xo_ref, h_ref, *, n_lat_tiles):
    is_lat = pl.program_id(0) < n_lat_tiles
    x0 = jnp.where(is_lat, x_ref[...] + pos_ref[...], ctx_ref[...])
    xo_ref[...] = x0
    h = _rms(x0, g_ref[...]) * (1.0 + mod_ref[1:2, :]) + mod_ref[0:1, :]
    h_ref[...] = h.astype(h_ref.dtype)


def embed_and_norm(x2, ctx2, pos, gain, mods, *, tr, seq, nbatch, h_dtype):
    n_lat, d = x2.shape
    n_ctx = ctx2.shape[0]
    t = n_lat + n_ctx
    assert n_lat % tr == 0 and n_ctx % tr == 0 and seq % tr == 0
    nl, npos = n_lat // tr, seq // tr
    row = lambda f: pl.BlockSpec((tr, d), f)
    return pl.pallas_call(
        functools.partial(_embed_kernel, n_lat_tiles=nl),
        name="embed_norm",
        out_shape=(jax.ShapeDtypeStruct((t, d), F32), jax.ShapeDtypeStruct((t, d), h_dtype)),
        grid=(t // tr,),
        in_specs=[row(lambda i: (jnp.minimum(i, nl - 1), 0)),
                  row(lambda i: (jnp.maximum(i - nl, 0), 0)),
                  row(lambda i: (i % npos, 0)),
                  pl.BlockSpec((None, 1, d), lambda i: (0, 0, 0)),
                  pl.BlockSpec((None, 8, d), _group_map(tr, seq, nbatch))],
        out_specs=(row(lambda i: (i, 0)), row(lambda i: (i, 0))),
        compiler_params=_params("parallel"),
    )(x2, ctx2, pos, gain, mods)


def _resid_kernel(*refs, gate_row, with_h):
    if with_h:
        x_ref, o_ref, go_ref, modc_ref, gn_ref, modn_ref, xo_ref, h_ref = refs
    else:
        x_ref, o_ref, go_ref, modc_ref, xo_ref = refs
    xn = x_ref[...] + modc_ref[gate_row:gate_row + 1, :] * _rms(o_ref[...].astype(F32), go_ref[...])
    xo_ref[...] = xn
    if with_h:
        shift_row = 3 if gate_row == 2 else 0
        h = _rms(xn, gn_ref[...]) * (1.0 + modn_ref[shift_row + 1:shift_row + 2, :]) \
            + modn_ref[shift_row:shift_row + 1, :]
        h_ref[...] = h.astype(h_ref.dtype)


def gated_residual(x, o, gain_o, mods_cur, gate_row, *, tr, seq, nbatch, rows=None,
                   gain_next=None, mods_next=None, h_dtype=None):
    t, d = x.shape
    rows = t if rows is None else rows
    assert rows % tr == 0
    with_h = gain_next is not None
    row = pl.BlockSpec((tr, d), lambda i: (i, 0))
    gain = pl.BlockSpec((None, 1, d), lambda i: (0, 0, 0))
    mod = pl.BlockSpec((None, 8, d), _group_map(tr, seq, nbatch))
    ins, in_specs = [x, o, gain_o, mods_cur], [row, row, gain, mod]
    out_shape, out_specs = [jax.ShapeDtypeStruct((rows, d), F32)], [row]
    if with_h:
        ins += [gain_next, mods_next]
        in_specs += [gain, mod]
        out_shape.append(jax.ShapeDtypeStruct((rows, d), h_dtype))
        out_specs.append(row)
    res = pl.pallas_call(
        functools.partial(_resid_kernel, gate_row=gate_row, with_h=with_h),
        name="gated_residual",
        out_shape=tuple(out_shape),
        grid=(rows // tr,),
        in_specs=in_specs,
        out_specs=tuple(out_specs),
        compiler_params=_params("parallel"),
    )(*ins)
    return res if with_h else res[0]


def _abmid_kernel(p_ref, cs_ref, lng_ref, lnb_ref, ws_ref, bs_ref, u_ref, v_ref, yb_ref, *, fw, sw):
    gw, sg = fw // MIX_GROUPS, sw // MIX_GROUPS
    cs = cs_ref[...]
    for g in range(MIX_GROUPS):
        uv = _dot(p_ref[:, g * gw:(g + 1) * gw].astype(BF16), cs)
        u_ref[:, g * gw:(g + 1) * gw] = uv[:, :gw].astype(BF16)
        v_ref[:, g * gw:(g + 1) * gw] = uv[:, gw:].astype(BF16)
    u = _gelu(p_ref[:, fw:fw + sw].astype(F32))
    gv = _gelu(p_ref[:, fw + sw:fw + 2 * sw].astype(F32))
    mu = jnp.mean(gv, axis=-1, keepdims=True)
    dv = gv - mu
    var = jnp.mean(dv * dv, axis=-1, keepdims=True)
    vn = (dv * lax.rsqrt(var + LN_EPS) * lng_ref[...] + lnb_ref[...]).astype(BF16)
    for g in range(MIX_GROUPS):
        s = _dot(ws_ref[g].astype(BF16), vn[:, g * sg:(g + 1) * sg]) + bs_ref[:, g:g + 1]
        yb_ref[:, g * sg:(g + 1) * sg] = (u[:, g * sg:(g + 1) * sg] * s).astype(BF16)


def ab_mid(p, cs, ln_g, ln_b, ws, bs_t, *, fw, sw):
    t = p.shape[0]
    assert t % SGU_CHUNK == 0
    full = lambda a: pl.BlockSpec(a.shape, lambda i: (0,) * a.ndim)
    out = lambda w: pl.BlockSpec((SGU_CHUNK, w), lambda i: (i, 0))
    return pl.pallas_call(
        functools.partial(_abmid_kernel, fw=fw, sw=sw),
        name="ab_mid",
        out_shape=(jax.ShapeDtypeStruct((t, fw), BF16), jax.ShapeDtypeStruct((t, fw), BF16),
                   jax.ShapeDtypeStruct((t, sw), BF16)),
        grid=(t // SGU_CHUNK,),
        in_specs=[pl.BlockSpec((SGU_CHUNK, p.shape[1]), lambda i: (i, 0)),
                  full(cs), full(ln_g), full(ln_b), full(ws), full(bs_t)],
        out_specs=(out(fw), out(fw), out(sw)),
        compiler_params=_params("parallel"),
    )(p, cs, ln_g, ln_b, ws, bs_t)


def _dft_cos_sin(n):
    blk = min(n, LANES)
    nb = n // blk
    k = jnp.arange(n, dtype=jnp.int32)[:, None]
    w = 2.0 * math.pi / n
    ang_t = w * ((k * jnp.arange(blk, dtype=jnp.int32)[None, :]) % n).astype(F32)
    ang_p = w * ((k * (jnp.arange(nb, dtype=jnp.int32) * blk)[None, :]) % n).astype(F32)
    ct, st = jnp.cos(ang_t)[:, None, :], jnp.sin(ang_t)[:, None, :]
    cp, sp = jnp.cos(ang_p)[:, :, None], jnp.sin(ang_p)[:, :, None]
    return (cp * ct - sp * st).reshape(n, n), (sp * ct + cp * st).reshape(n, n)


def _dftmat_kernel(ct_ref, st_ref, cp_ref, sp_ref, c_ref, s_ref):
    ct, st = ct_ref[...], st_ref[...]
    cp, sp = cp_ref[...], sp_ref[...]
    c_ref[...] = (cp * ct - sp * st).astype(c_ref.dtype)
    s_ref[...] = (-(sp * ct + cp * st)).astype(s_ref.dtype)


def dft_position_operators(n, *, tr=256, tc=2048):
    tr, tc = min(tr, n), min(tc, n)
    assert n % tr == 0 and n % tc == 0
    w = 2.0 * math.pi / n
    l = jnp.arange(n, dtype=jnp.int32)[None, :]
    ang_t = w * ((jnp.arange(tr, dtype=jnp.int32)[:, None] * l) % n).astype(F32)
    ang_p = w * (((jnp.arange(n // tr, dtype=jnp.int32) * tr)[:, None] * l) % n).astype(F32)
    table = pl.BlockSpec((tr, tc), lambda j, i: (0, j))
    phase = pl.BlockSpec((None, 1, tc), lambda j, i: (i, 0, j))
    out = pl.BlockSpec((tr, tc), lambda j, i: (i, j))
    return pl.pallas_call(
        _dftmat_kernel,
        name="dft_operators",
        out_shape=(jax.ShapeDtypeStruct((n, n), BF16), jax.ShapeDtypeStruct((n, n), BF16)),
        grid=(n // tc, n // tr),
        in_specs=[table, table, phase, phase],
        out_specs=(out, out),
        compiler_params=_params("parallel", "parallel"),
    )(jnp.cos(ang_t), jnp.sin(ang_t), jnp.cos(ang_p)[:, None, :], jnp.sin(ang_p)[:, None, :])


def _shiftmix_kernel(cur_ref, up_ref, dn_ref, mu_ref, *out_refs, tr, n_lat_tiles, lat_tiles_per_seq,
                     ctx_tiles_per_seq, d):
    i = pl.program_id(0)
    rows = lax.broadcasted_iota(jnp.int32, (tr, 1), 0)

    def emit(lo, hi, shifted):
        h = cur_ref[:, lo:hi]
        xx = shifted - h
        for m, o_ref in enumerate(out_refs):
            o_ref[:, lo:hi] = (h + xx * mu_ref[m:m + 1, lo:hi]).astype(o_ref.dtype)

    @pl.when(i < n_lat_tiles)
    def _():
        q = d // 4
        col = rows & (GRID_W - 1)
        ib = i % lat_tiles_per_seq
        up_ok = (ib > 0).astype(F32)
        dn_ok = (ib < lat_tiles_per_seq - 1).astype(F32)
        emit(0, q, jnp.where(col > 0, pltpu.roll(cur_ref[:, 0:q], 1, 0), 0.0))
        emit(q, 2 * q, jnp.where(col < GRID_W - 1, pltpu.roll(cur_ref[:, q:2 * q], tr - 1, 0), 0.0))
        emit(2 * q, 3 * q, jnp.concatenate(
            [up_ref[:, 2 * q:3 * q] * up_ok, cur_ref[0:tr - GRID_W, 2 * q:3 * q]], axis=0))
        emit(3 * q, d, jnp.concatenate(
            [cur_ref[GRID_W:tr, 3 * q:d], dn_ref[:, 3 * q:d] * dn_ok], axis=0))

    @pl.when(i >= n_lat_tiles)
    def _():
        hf = d // 2
        jb = (i - n_lat_tiles) % ctx_tiles_per_seq
        up_ok = (jb > 0).astype(F32)
        dn_ok = (jb < ctx_tiles_per_seq - 1).astype(F32)
        prev = jnp.where(rows == 0, up_ref[GRID_W - 1:GRID_W, 0:hf] * up_ok,
                         pltpu.roll(cur_ref[:, 0:hf], 1, 0))
        nxt = jnp.where(rows == tr - 1, dn_ref[0:1, hf:d] * dn_ok,
                        pltpu.roll(cur_ref[:, hf:d], tr - 1, 0))
        emit(0, hf, prev)
        emit(hf, d, nxt)


def shift_mix(h, mu8, *, tr, n_lat, seq, ctx_len):
    t, d = h.shape
    assert t % tr == 0 and tr % GRID_W == 0 and seq % tr == 0 and ctx_len % tr == 0 and tr > GRID_W
    per = tr // GRID_W
    nhalo = t // GRID_W
    halo = lambda f: pl.BlockSpec((GRID_W, d), f)
    row = pl.BlockSpec((tr, d), lambda i: (i, 0))
    kern = functools.partial(_shiftmix_kernel, tr=tr, n_lat_tiles=n_lat // tr,
                             lat_tiles_per_seq=seq // tr, ctx_tiles_per_seq=ctx_len // tr, d=d)
    return pl.pallas_call(
        kern,
        name="shift_mix",
        out_shape=tuple(jax.ShapeDtypeStruct((t, d), BF16) for _ in range(6)),
        grid=(t // tr,),
        in_specs=[row,
                  halo(lambda i: (jnp.maximum(i * per - 1, 0), 0)),
                  halo(lambda i: (jnp.minimum((i + 1) * per, nhalo - 1), 0)),
                  pl.BlockSpec((8, d), lambda i: (0, 0))],
        out_specs=tuple(row for _ in range(6)),
        compiler_params=_params("parallel"),
    )(h, h, h, mu8)


def _headsum(x, bd):
    r, d = x.shape
    nt = d // LANES
    xs = jnp.concatenate([x[:, t * LANES:(t + 1) * LANES] for t in range(nt)], axis=0)
    hi = xs.astype(BF16)
    lo = (xs - hi.astype(F32)).astype(BF16)
    s = _dot(hi, bd) + _dot(lo, bd)
    return jnp.concatenate([s[t * r:(t + 1) * r] for t in range(nt)], axis=1)


def _prep_kernel(k_ref, hw_ref, ha_ref, w2_ref, a2_ref, w0_ref, a0_ref, kkp_ref, bd_ref,
                 kk_ref, lw0_ref, lw1_ref, as0_ref, as1_ref, *, lora):
    for dr, (lw_ref, as_ref) in enumerate(((lw0_ref, as0_ref), (lw1_ref, as1_ref))):
        z = _dot(hw_ref[:, dr * lora:(dr + 1) * lora].astype(BF16), w2_ref[dr]) + w0_ref[dr:dr + 1, :]
        lw_ref[...] = -DECAY_SCALE * jax.nn.sigmoid(z)
        za = _dot(ha_ref[:, dr * lora:(dr + 1) * lora].astype(BF16), a2_ref[dr]) + a0_ref[dr:dr + 1, :]
        as_ref[...] = jax.nn.sigmoid(za).astype(as_ref.dtype)
    kq = k_ref[...] * kkp_ref[...]
    kk_ref[...] = kq * lax.rsqrt(_headsum(kq * kq, bd_ref[...]) + L2_EPS)


def rwkv_prep(k, hw, ha, w2, a2, w0, a0, k_k, bd, *, tr, lora):
    t, d = k.shape
    assert t % tr == 0
    row = pl.BlockSpec((tr, d), lambda i: (i, 0))
    lrow = pl.BlockSpec((tr, 2 * lora), lambda i: (i, 0))
    full = lambda a: pl.BlockSpec(a.shape, lambda i: (0,) * a.ndim)
    return pl.pallas_call(
        functools.partial(_prep_kernel, lora=lora),
        name="rwkv_prep",
        out_shape=tuple(jax.ShapeDtypeStruct((t, d), dt) for dt in (F32, F32, F32, BF16, BF16)),
        grid=(t // tr,),
        in_specs=[row, lrow, lrow, full(w2), full(a2), full(w0), full(a0), full(k_k), full(bd)],
        out_specs=tuple(row for _ in range(5)),
        compiler_params=_params("parallel"),
    )(k, hw, ha, w2, a2, w0, a0, k_k, bd)


def _block_diag(x, mask):
    tiled = jnp.concatenate([x] * HEADS_PER_PACK, axis=0)
    return jnp.where(mask, tiled, 0.0).astype(BF16)


def _scan_kernel(r_ref, k_ref, v_ref, kk_ref, as_ref, lw_ref, ka_ref, y_ref, s_ref, *, reverse, npack,
                 nsub):
    c = SCAN_CHUNK

    @pl.when(pl.program_id(2) == 0)
    def _():
        s_ref[...] = jnp.zeros_like(s_ref)

    def earlier(a, b, strict):
        if reverse:
            return (a > b) if strict else (a >= b)
        return (a < b) if strict else (a <= b)

    ti = lax.broadcasted_iota(jnp.int32, (c, c), 0)
    tj = lax.broadcasted_iota(jnp.int32, (c, c), 1)
    tri = jnp.where(earlier(tj, ti, False), 1.0, 0.0).astype(BF16)
    pt = lax.broadcasted_iota(jnp.int32, (c, PACK), 0)
    pj = lax.broadcasted_iota(jnp.int32, (c, PACK), 1) & (HEAD - 1)
    strict_m = earlier(pj, pt, True)
    incl_m = earlier(pj, pt, False)
    eye_m = pj == pt
    level_m = [((pt >> (l + 1)) == (pj >> (l + 1))) & ((pt >> l) != (pj >> l))
               for l in range(c.bit_length() - 1)]
    hshift = HEAD.bit_length() - 1
    br = lax.broadcasted_iota(jnp.int32, (PACK, PACK), 0) >> hshift
    bc = lax.broadcasted_iota(jnp.int32, (PACK, PACK), 1) >> hshift
    bmask = br == bc
    last = 0 if reverse else c - 1

    packs = range(npack)
    sls = [slice(q * PACK, (q + 1) * PACK) for q in packs]
    each = lambda f, *cols: [f(*args) for args in zip(*cols)]

    def chunk(rows):
        def cum_logdecay(sl):
            lw = lw_ref[rows, sl]
            hi = lw.astype(BF16)
            lo = (lw - hi.astype(F32)).astype(BF16)
            return _dot(tri, hi) + _dot(tri, lo)

        cl = each(cum_logdecay, sls)
        tot = [x[last:last + 1, :] for x in cl]
        e_neg = [jnp.exp(-x) for x in cl]
        asg = [as_ref[rows, sl].astype(F32) for sl in sls]
        kd = [k_ref[rows, sl] * (1.0 + (a - 1.0) * ka_ref[:, sl]) for sl, a in zip(sls, asg)]
        bq = [kk_ref[rows, sl] * a for sl, a in zip(sls, asg)]
        lhs = [jnp.concatenate([-kk_ref[rows, sl] * jnp.exp(x - lw_ref[rows, sl]),
                                r_ref[rows, sl] * jnp.exp(x)], axis=0).astype(BF16)
               for sl, x in zip(sls, cl)]
        mb = each(lambda l, b, e: _dot_nt(l, _block_diag(b * e, bmask)), lhs, bq, e_neg)
        mk = each(lambda l, k, e: _dot_nt(l, _block_diag(k * e, bmask)), lhs, kd, e_neg)
        a_s = [_dot_nt(l, s_ref[q].astype(BF16)) for q, l in zip(packs, lhs)]
        v_bd = [_block_diag(v_ref[rows, sl], bmask) for sl in sls]
        w_in = each(lambda a, m, vb: a[:c] + _dot(jnp.where(strict_m, m[:c], 0.0).astype(BF16), vb),
                    a_s, mk, v_bd)
        mab = [jnp.where(strict_m, m[:c], 0.0) for m in mb]
        tinv = [jnp.where(eye_m, 1.0, 0.0) + jnp.where(level_m[0], m, 0.0) for m in mab]
        for lm in level_m[1:]:
            dx = each(lambda t, m: _dot(t.astype(BF16), _block_diag(jnp.where(lm, m, 0.0), bmask)),
                      tinv, mab)
            tinv = each(lambda t, x: t + _dot(x.astype(BF16), _block_diag(t, bmask)), tinv, dx)
        u = each(lambda t, w: _dot(t.astype(BF16), _block_diag(w, bmask)), tinv, w_in)
        for sl, a, b, k, x, vb in zip(sls, a_s, mb, mk, u, v_bd):
            y = a[c:] + _dot(jnp.where(incl_m, b[c:], 0.0).astype(BF16), _block_diag(x, bmask)) \
                + _dot(jnp.where(incl_m, k[c:], 0.0).astype(BF16), vb)
            y_ref[rows, sl] = y.astype(y_ref.dtype)
        for q, sl, x, b, k, tt, cc in zip(packs, sls, u, bq, kd, tot, cl):
            e_end = jnp.exp(tt - cc)
            uv_t = jnp.transpose(jnp.concatenate([x, v_ref[rows, sl]], axis=0)).astype(BF16)
            bk = jnp.concatenate([b * e_end, k * e_end], axis=0).astype(BF16)
            s_ref[q] = s_ref[q] * jnp.exp(tt) + jnp.where(bmask, _dot(uv_t, bk), 0.0)

    subs = range(nsub - 1, -1, -1) if reverse else range(nsub)
    for sub in subs:
        chunk(slice(sub * c, (sub + 1) * c))


def wkv_scan(r, k, v, kk, a_sig, lw, k_a, *, reverse, nbatch, seq, ctx_len, lane_block,
             chunks_per_step=2):
    t, d = r.shape
    c = SCAN_CHUNK * chunks_per_step
    lane_block = min(lane_block, d)
    assert seq % c == 0 and ctx_len % c == 0 and d % lane_block == 0 and lane_block % PACK == 0
    n_lat_c, n_ctx_c = seq // c, ctx_len // c
    ctx0 = nbatch * n_lat_c

    def blk(b, hg, ci):
        cc = (n_ctx_c - 1 - ci) if reverse else ci
        lc = ci - n_ctx_c
        lc = (n_lat_c - 1 - lc) if reverse else lc
        return (jnp.where(ci < n_ctx_c, ctx0 + b * n_ctx_c + cc, b * n_lat_c + lc), hg)

    row = pl.BlockSpec((c, lane_block), blk)
    npack = lane_block // PACK
    return pl.pallas_call(
        functools.partial(_scan_kernel, reverse=reverse, npack=npack, nsub=chunks_per_step),
        name="wkv_scan_rev" if reverse else "wkv_scan_fwd",
        out_shape=jax.ShapeDtypeStruct((t, d), BF16),
        grid=(nbatch, d // lane_block, n_ctx_c + n_lat_c),
        in_specs=[row] * 6 + [pl.BlockSpec((1, lane_block), lambda b, hg, ci: (0, hg))],
        out_specs=row,
        scratch_shapes=[pltpu.VMEM((npack, PACK, PACK), F32)],
        compiler_params=_params("parallel", "parallel", "arbitrary"),
    )(r, k, v, kk, a_sig, lw, k_a)


def _rwkv_out_kernel(y0_ref, y1_ref, r_ref, k_ref, v_ref, as0_ref, as1_ref, g_ref,
                     ka_ref, rk_ref, lng_ref, lnb_ref, bd_ref, o_ref):
    bd = bd_ref[...]
    y = y0_ref[...].astype(F32) + y1_ref[...].astype(F32)
    inv = 1.0 / HEAD
    dy = y - _headsum(y, bd) * inv
    yn = dy * lax.rsqrt(_headsum(dy * dy, bd) * inv + GN_EPS) * lng_ref[...] + lnb_ref[...]
    a_mean = 0.5 * (as0_ref[...].astype(F32) + as1_ref[...].astype(F32))
    k_bonus = k_ref[...] * (1.0 + (a_mean - 1.0) * ka_ref[...])
    bonus = _headsum(r_ref[...] * k_bonus * rk_ref[...], bd) * v_ref[...]
    o_ref[...] = ((yn + bonus) * g_ref[...].astype(F32)).astype(o_ref.dtype)


def rwkv_out(y0, y1, r, k, v, as0, as1, g, k_a, r_k, ln_g, ln_b, bd, *, tr):
    t, d = r.shape
    assert t % tr == 0
    row = pl.BlockSpec((tr, d), lambda i: (i, 0))
    full = lambda a: pl.BlockSpec(a.shape, lambda i: (0,) * a.ndim)
    return pl.pallas_call(
        _rwkv_out_kernel,
        name="rwkv_out",
        out_shape=jax.ShapeDtypeStruct((t, d), BF16),
        grid=(t // tr,),
        in_specs=[row] * 8 + [full(k_a), full(r_k), full(ln_g), full(ln_b), full(bd)],
        out_specs=row,
        compiler_params=_params("parallel"),
    )(y0, y1, r, k, v, as0, as1, g, k_a, r_k, ln_g, ln_b, bd)


def _pad_to(a, axis, size):
    pad = [(0, 0)] * a.ndim
    pad[axis] = (0, size - a.shape[axis])
    return jnp.pad(a, pad)


def _round_up(n, m):
    return -(-n // m) * m


def kernel(x, c, ctx, c_ctx, mod_w1, mod_w2, mod_b, norm_g, ffn_w1, ffn_w3, ffn_w2, ab_w_in, ab_w_out, sgu_ln_g, sgu_ln_b, sgu_ws, sgu_bs, rwkv_mu, rwkv_w_rkv, rwkv_w_o, rwkv_w0, rwkv_w1, rwkv_w2, rwkv_a0, rwkv_a1, rwkv_a2, rwkv_g1, rwkv_g2, rwkv_k_k, rwkv_k_a, rwkv_r_k, rwkv_lnx_g, rwkv_lnx_b):
    nbatch, seq, d = x.shape
    ctx_len = ctx.shape[1]
    depth = mod_w1.shape[0]
    n_lat, n_ctx = nbatch * seq, nbatch * ctx_len
    sw = sgu_ln_g.shape[1]
    fw = ab_w_in.shape[2] - 2 * sw
    gw = fw // MIX_GROUPS
    lora = rwkv_w1.shape[3]
    tr = 256
    tm = 512
    tm_ffn = next(t for t in (1536, 1024, 768, 512, 256) if (n_lat + n_ctx) % t == 0)
    common = dict(tr=tr, seq=seq, nbatch=nbatch)

    cond = _pad_to(jax.nn.silu(jnp.concatenate([c, c_ctx[None, :]], axis=0)), 0, 8)
    mods = []
    for layer in range(depth):
        hid = matmul([cond], [(mod_w1, (layer,))], tm=8, tn=mod_w1.shape[2])
        m = matmul([hid], [(mod_w2, (layer,))], tm=8, tn=2048) + mod_b[layer]
        mods.append(_pad_to(m[:nbatch + 1].reshape(nbatch + 1, 6, d), 1, 8))
    gains = norm_g.reshape(depth * 4, 1, d)
    gain = lambda layer, j: gains[layer * 4 + j:layer * 4 + j + 1]

    quarter = d // 4
    omega = POS_BASE ** (-jnp.arange(quarter, dtype=F32) / quarter)
    ang_r = jnp.arange(seq // GRID_W, dtype=F32)[:, None] * omega
    ang_c = jnp.arange(GRID_W, dtype=F32)[:, None] * omega
    pos = jnp.concatenate(
        [jnp.repeat(jnp.concatenate([jnp.sin(ang_r), jnp.cos(ang_r)], axis=-1), GRID_W, axis=0),
         jnp.tile(jnp.concatenate([jnp.sin(ang_c), jnp.cos(ang_c)], axis=-1), (seq // GRID_W, 1))], axis=-1)

    cc, sc = _dft_cos_sin(gw)
    cs_chan = jnp.concatenate([cc, sc], axis=1).astype(BF16)
    ctx_dft = dft_position_operators(ctx_len)
    assert fw == sw
    lane = jnp.arange(LANES)
    bd = (lane[:, None] // HEAD == lane[None, :] // HEAD).astype(BF16)

    dff = ffn_w1.shape[2]
    w_in_bf, w_out_bf = ab_w_in.astype(BF16), ab_w_out.astype(BF16)
    w_rkv_bf, w_o_bf = rwkv_w_rkv.astype(BF16), rwkv_w_o.astype(BF16)
    ffn1_bf, ffn3_bf, ffn2_bf = ffn_w1.astype(BF16), ffn_w3.astype(BF16), ffn_w2.astype(BF16)
    xs, h = embed_and_norm(x.reshape(n_lat, d), ctx.reshape(n_ctx, d), pos, gain(0, 0), mods[0],
                           h_dtype=BF16, **common)

    for layer in range(depth):
        i = layer // 2
        last = layer == depth - 1
        if layer % 2 == 0:
            p = matmul([h], [(w_in_bf, (i,))], tm=tm, tn=1024, out_dtype=BF16, name="mm_ab_in")
            u, v, yb = ab_mid(p, cs_chan, sgu_ln_g[i][None, :], sgu_ln_b[i][None, :], sgu_ws[i],
                              sgu_bs[i].T, fw=fw, sw=sw)
            ya = jnp.concatenate(
                [dft_positions_two_stage(u, v, batch=b, length=seq, scale=1.0 / math.sqrt(seq * gw))
                 for b in range(nbatch)]
                + [dft_positions(u, v, *ctx_dft, nbatch=nbatch, length=ctx_len, row0=n_lat,
                                 scale=1.0 / math.sqrt(ctx_len * gw), tm=1024, tn=1024, tk=2048)], axis=0)
            o = matmul([ya, yb], [(w_out_bf, (i,), 0), (w_out_bf, (i,), 1)], tm=tm, tn=1024,
                       out_dtype=BF16, name="mm_ab_out")
        else:
            mixes = shift_mix(h, _pad_to(rwkv_mu[i], 0, 8), tr=128, n_lat=n_lat, seq=seq, ctx_len=ctx_len)
            xr, xw, xk, xv, xa, xg = mixes
            r = matmul([xr], [(w_rkv_bf, (i, 0))], tm=tm, tn=1024)
            k = matmul([xk], [(w_rkv_bf, (i, 1))], tm=tm, tn=1024)
            v = matmul([xv], [(w_rkv_bf, (i, 2))], tm=tm, tn=1024)
            hw = matmul([xw], [jnp.concatenate([rwkv_w1[i, 0], rwkv_w1[i, 1]], axis=1).astype(BF16)],
                        tm=tm, tn=2 * lora, act="tanh")
            ha = matmul([xa], [jnp.concatenate([rwkv_a1[i, 0], rwkv_a1[i, 1]], axis=1).astype(BF16)],
                        tm=tm, tn=2 * lora)
            glp = _round_up(rwkv_g1.shape[2], LANES)
            hg = matmul([xg], [_pad_to(rwkv_g1[i], 1, glp).astype(BF16)], tm=tm, tn=glp,
                        act="sigmoid", out_dtype=BF16)
            g = matmul([hg], [_pad_to(rwkv_g2[i], 0, glp).astype(BF16)], tm=tm, tn=1024, out_dtype=BF16)
            k_a = rwkv_k_a[i][None, :]
            kk, lw0, lw1, as0, as1 = rwkv_prep(
                k, hw, ha, rwkv_w2[i].astype(BF16), rwkv_a2[i].astype(BF16), rwkv_w0[i], rwkv_a0[i],
                rwkv_k_k[i][None, :], bd, tr=128, lora=lora)
            scan = functools.partial(wkv_scan, r, k, v, kk, nbatch=nbatch, seq=seq, ctx_len=ctx_len,
                                     lane_block=2048)
            y0 = scan(as0, lw0, k_a, reverse=False)
            y1 = scan(as1, lw1, k_a, reverse=True)
            ob = rwkv_out(y0, y1, r, k, v, as0, as1, g, k_a, rwkv_r_k[i].reshape(1, d),
                          rwkv_lnx_g[i][None, :], rwkv_lnx_b[i][None, :], bd, tr=128)
            o = matmul([ob], [(w_o_bf, (i,))], tm=tm, tn=1024, out_dtype=BF16, name="mm_rwkv_out")

        xs, h = gated_residual(xs, o, gain(layer, 1), mods[layer], 2, gain_next=gain(layer, 2),
                               mods_next=mods[layer], h_dtype=BF16, **common)
        hid = swiglu_up(h, (ffn1_bf, (layer,)), (ffn3_bf, (layer,)), tm=tm_ffn, tn=256)
        f = matmul([hid], [(ffn2_bf, (layer,))], tm=tm_ffn // 2, tn=1024, tk=dff // 2,
                   out_dtype=BF16, name="mm_ffn_down")
        if last:
            xs = gated_residual(xs, f, gain(layer, 3), mods[layer], 5, rows=n_lat, **common)
        else:
            xs, h = gated_residual(xs, f, gain(layer, 3), mods[layer], 5, gain_next=gain(layer + 1, 0),
                                   mods_next=mods[layer + 1],
                                   h_dtype=F32 if (layer + 1) % 2 else BF16, **common)
    return xs.reshape(nbatch, seq, d)
```

```python
import functools
import math

import jax
import jax.numpy as jnp
from jax import lax
from jax.experimental import pallas as pl
from jax.experimental.pallas import tpu as pltpu

F32 = jnp.float32
BF16 = jnp.bfloat16

GRID_W = 64
SGU_CHUNK = 128
MIX_GROUPS = 4
HEAD = 64
RMS_EPS, LN_EPS, GN_EPS, L2_EPS = 1e-6, 1e-5, 64e-5, 1e-12
POS_BASE = 10000.0
DECAY_SCALE = math.exp(-0.5)

LANES = 128
MXU_DIM = 256
SCAN_CHUNK = 64
PACK = LANES
HEADS_PER_PACK = PACK // HEAD
VMEM_LIMIT_BYTES = 48 * 1024 * 1024


def _params(*sem):
    return pltpu.CompilerParams(dimension_semantics=sem, vmem_limit_bytes=VMEM_LIMIT_BYTES)


def _fit(n, t):
    t = min(t, n)
    while n % t:
        t -= LANES
    return t


def _dot(a, b):
    return jnp.dot(a, b, preferred_element_type=F32)


def _dot_nt(a, b):
    return lax.dot_general(a, b, (((1,), (1,)), ((), ())), preferred_element_type=F32)


def _act(x, act):
    if act == "tanh":
        return jnp.tanh(x)
    if act == "sigmoid":
        return jax.nn.sigmoid(x)
    assert act is None
    return x


def _gelu(x):
    return 0.5 * x * (1.0 + jnp.tanh(math.sqrt(2.0 / math.pi) * (x + 0.044715 * (x * x * x))))


def _rms(x, g):
    return x * lax.rsqrt(jnp.mean(x * x, axis=-1, keepdims=True) + RMS_EPS) * g


def _mm_kernel(*refs, npair, nk, kaxis, act, scale):
    o_ref = refs[2 * npair]

    def product():
        part = None
        for a_ref, b_ref in zip(refs[:npair], refs[npair:2 * npair]):
            d = _dot(a_ref[...].astype(BF16), b_ref[...].astype(BF16))
            part = d if part is None else part + d
        return part

    def finish(acc):
        if scale is not None:
            acc = acc * scale
        o_ref[...] = _act(acc, act).astype(o_ref.dtype)

    if nk == 1:
        finish(product())
        return
    acc_ref = refs[2 * npair + 1]
    k = pl.program_id(kaxis)

    @pl.when(k == 0)
    def _():
        acc_ref[...] = jnp.zeros_like(acc_ref)

    acc_ref[...] += product()

    @pl.when(k == nk - 1)
    def _():
        finish(acc_ref[...])


def _weight(b):
    if isinstance(b, tuple):
        return b[0], tuple(b[1]), (b[2] if len(b) > 2 else 0)
    return b, (), 0


def _weight_spec(b, tk, tn, kj):
    arr, lead, k0 = _weight(b)
    assert arr.ndim == len(lead) + 2
    return pl.BlockSpec((None,) * len(lead) + (tk, tn), lambda *g: lead + (kj(*g)[0] + k0, kj(*g)[1]))


def matmul(a_list, b_list, *, tm, tn, tk=None, out_dtype=F32, act=None, scale=None, name="mm"):
    m, kdim = a_list[0].shape
    n = _weight(b_list[0])[0].shape[-1]
    tk = kdim if tk is None else tk
    tm, tn = min(tm, m), _fit(n, tn)
    assert m % tm == 0 and n % tn == 0 and kdim % tk == 0, (m, n, kdim, tm, tn, tk)
    nk = kdim // tk
    npair = len(a_list)
    kern = functools.partial(_mm_kernel, npair=npair, nk=nk, kaxis=2, act=act, scale=scale)
    return pl.pallas_call(
        kern,
        name=name,
        out_shape=jax.ShapeDtypeStruct((m, n), out_dtype),
        grid=(m // tm, n // tn, nk),
        in_specs=[pl.BlockSpec((tm, tk), lambda i, j, k: (i, k)) for _ in a_list]
        + [_weight_spec(b, tk, tn, lambda i, j, k: (k, j)) for b in b_list],
        out_specs=pl.BlockSpec((tm, tn), lambda i, j, k: (i, j)),
        scratch_shapes=[pltpu.VMEM((tm, tn), F32)] if nk > 1 else [],
        compiler_params=_params("parallel", "parallel", "arbitrary"),
    )(*a_list, *[_weight(b)[0] for b in b_list])


def dft_positions(u, v, cmat, smat_neg, *, nbatch, length, row0, scale, tm, tn, tk):
    width = u.shape[1]
    tm, tk, tn = min(tm, length), min(tk, length), min(tn, width)
    assert length % tm == 0 and length % tk == 0 and width % tn == 0 and row0 % tk == 0
    nk, ni = length // tk, length // tm
    kern = functools.partial(_mm_kernel, npair=2, nk=nk, kaxis=3, act=None, scale=scale)
    a_spec = pl.BlockSpec((tm, tk), lambda b, i, j, k: (i, k))
    b_spec = pl.BlockSpec((tk, tn), lambda b, i, j, k: (row0 // tk + b * nk + k, j))
    return pl.pallas_call(
        kern,
        name="dft_positions",
        out_shape=jax.ShapeDtypeStruct((nbatch * length, width), F32),
        grid=(nbatch, ni, width // tn, nk),
        in_specs=[a_spec, a_spec, b_spec, b_spec],
        out_specs=pl.BlockSpec((tm, tn), lambda b, i, j, k: (b * ni + i, j)),
        scratch_shapes=[pltpu.VMEM((tm, tn), F32)] if nk > 1 else [],
        compiler_params=_params("parallel", "parallel", "parallel", "arbitrary"),
    )(cmat, smat_neg, u, v)


def _dft_stage2_kernel(gr_ref, gi_ref, zr_ref, zi_ref, o_ref, *, scale):
    y = (_dot(gr_ref[...], zr_ref[...]) + _dot(gi_ref[...], zi_ref[...])) * scale
    o_ref[...] = y.reshape(o_ref.shape).astype(o_ref.dtype)


def dft_positions_two_stage(u, v, *, batch, length, scale):
    width = u.shape[1]
    n2 = LANES
    n1 = length // n2
    sub = 8
    assert length % n2 == 0 and n1 % sub == 0 and u.shape[0] >= (batch + 1) * length
    c2, s2 = _dft_cos_sin(n2)
    a_u = jnp.concatenate([c2, -s2], axis=0).astype(BF16)
    a_v = jnp.concatenate([-s2, -c2], axis=0).astype(BF16)
    nb = u.shape[0] // length
    view = lambda a: a[:nb * length].reshape(nb, n2, n1 * width)
    z = matmul([a_u, a_v], [(view(u), (batch,)), (view(v), (batch,))], tm=2 * n2, tn=8192,
               out_dtype=BF16, name="dft_stage1")
    z = z.reshape(2, n2 * n1, width)
    t = jnp.arange(n2 // sub, dtype=jnp.int32)[:, None, None, None]
    k1 = jnp.arange(n1, dtype=jnp.int32)[None, :, None, None]
    j = jnp.arange(sub, dtype=jnp.int32)[None, None, :, None]
    l1 = jnp.arange(n1, dtype=jnp.int32)[None, None, None, :]
    ang = (2.0 * math.pi / length) * ((l1 * (n2 * k1 + sub * t + j)) % length).astype(F32)
    eye = jnp.eye(sub, dtype=F32)[None, None, :, :, None]
    op = lambda f: (f(ang)[:, :, :, None, :] * eye).reshape(n2 // sub, sub * n1, sub * n1).astype(BF16)
    g_spec = pl.BlockSpec((None, sub * n1, sub * n1), lambda i: (i, 0, 0))
    z_spec = lambda part: pl.BlockSpec((None, sub * n1, width), lambda i: (part, i, 0))
    out = pl.pallas_call(
        functools.partial(_dft_stage2_kernel, scale=scale),
        name="dft_stage2",
        out_shape=jax.ShapeDtypeStruct((n1, n2, width), F32),
        grid=(n2 // sub,),
        in_specs=[g_spec, g_spec, z_spec(0), z_spec(1)],
        out_specs=pl.BlockSpec((n1, sub, width), lambda i: (0, i, 0)),
        compiler_params=_params("parallel"),
    )(op(jnp.cos), op(jnp.sin), z, z)
    return out.reshape(length, width)


def _swiglu_kernel(h_ref, w1_ref, w3_ref, o_ref):
    h = h_ref[...]
    a = _dot(h, w1_ref[...])
    b = _dot(h, w3_ref[...])
    o_ref[...] = (a * jax.nn.sigmoid(a) * b).astype(o_ref.dtype)


def swiglu_up(h, w1, w3, *, tm, tn):
    m, kdim = h.shape
    n = _weight(w1)[0].shape[-1]
    tm, tn = min(tm, m), min(tn, n)
    assert m % tm == 0 and n % tn == 0
    w_spec = lambda w: _weight_spec(w, kdim, tn, lambda i, j: (0, j))
    return pl.pallas_call(
        _swiglu_kernel,
        name="swiglu_up",
        out_shape=jax.ShapeDtypeStruct((m, n), BF16),
        grid=(m // tm, n // tn),
        in_specs=[pl.BlockSpec((tm, kdim), lambda i, j: (i, 0)), w_spec(w1), w_spec(w3)],
        out_specs=pl.BlockSpec((tm, tn), lambda i, j: (i, j)),
        compiler_params=_params("parallel", "parallel"),
    )(h, _weight(w1)[0], _weight(w3)[0])


def _group_map(tr, seq, nbatch):
    return lambda i: (jnp.minimum((i * tr) // seq, nbatch), 0, 0)


def _embed_kernel(x_ref, ctx_ref, pos_ref, g_ref, mod_ref, xo_ref, h_ref, *, n_lat_tiles):
    is_lat = pl.program_id(0) < n_lat_tiles
    x0 = jnp.where(is_lat, x_ref[...] + pos_ref[...], ctx_ref[...])
    xo_ref[...] = x0
    h = _rms(x0, g_ref[...]) * (1.0 + mod_ref[1:2, :]) + mod_ref[0:1, :]
    h_ref[...] = h.astype(h_ref.dtype)


def embed_and_norm(x2, ctx2, pos, gain, mods, *, tr, seq, nbatch, h_dtype):
    n_lat, d = x2.shape
    n_ctx = ctx2.shape[0]
    t = n_lat + n_ctx
    assert n_lat % tr == 0 and n_ctx % tr == 0 and seq % tr == 0
    nl, npos = n_lat // tr, seq // tr
    row = lambda f: pl.BlockSpec((tr, d), f)
    return pl.pallas_call(
        functools.partial(_embed_kernel, n_lat_tiles=nl),
        name="embed_norm",
        out_shape=(jax.ShapeDtypeStruct((t, d), F32), jax.ShapeDtypeStruct((t, d), h_dtype)),
        grid=(t // tr,),
        in_specs=[row(lambda i: (jnp.minimum(i, nl - 1), 0)),
                  row(lambda i: (jnp.maximum(i - nl, 0), 0)),
                  row(lambda i: (i % npos, 0)),
                  pl.BlockSpec((None, 1, d), lambda i: (0, 0, 0)),
                  pl.BlockSpec((None, 8, d), _group_map(tr, seq, nbatch))],
        out_specs=(row(lambda i: (i, 0)), row(lambda i: (i, 0))),
        compiler_params=_params("parallel"),
    )(x2, ctx2, pos, gain, mods)


def _resid_kernel(*refs, gate_row, with_h):
    if with_h:
        x_ref, o_ref, go_ref, modc_ref, gn_ref, modn_ref, xo_ref, h_ref = refs
    else:
        x_ref, o_ref, go_ref, modc_ref, xo_ref = refs
    xn = x_ref[...] + modc_ref[gate_row:gate_row + 1, :] * _rms(o_ref[...].astype(F32), go_ref[...])
    xo_ref[...] = xn
    if with_h:
        shift_row = 3 if gate_row == 2 else 0
        h = _rms(xn, gn_ref[...]) * (1.0 + modn_ref[shift_row + 1:shift_row + 2, :]) \
            + modn_ref[shift_row:shift_row + 1, :]
        h_ref[...] = h.astype(h_ref.dtype)


def gated_residual(x, o, gain_o, mods_cur, gate_row, *, tr, seq, nbatch, rows=None,
                   gain_next=None, mods_next=None, h_dtype=None):
    t, d = x.shape
    rows = t if rows is None else rows
    assert rows % tr == 0
    with_h = gain_next is not None
    row = pl.BlockSpec((tr, d), lambda i: (i, 0))
    gain = pl.BlockSpec((None, 1, d), lambda i: (0, 0, 0))
    mod = pl.BlockSpec((None, 8, d), _group_map(tr, seq, nbatch))
    ins, in_specs = [x, o, gain_o, mods_cur], [row, row, gain, mod]
    out_shape, out_specs = [jax.ShapeDtypeStruct((rows, d), F32)], [row]
    if with_h:
        ins += [gain_next, mods_next]
        in_specs += [gain, mod]
        out_shape.append(jax.ShapeDtypeStruct((rows, d), h_dtype))
        out_specs.append(row)
    res = pl.pallas_call(
        functools.partial(_resid_kernel, gate_row=gate_row, with_h=with_h),
        name="gated_residual",
        out_shape=tuple(out_shape),
        grid=(rows // tr,),
        in_specs=in_specs,
        out_specs=tuple(out_specs),
        compiler_params=_params("parallel"),
    )(*ins)
    return res if with_h else res[0]


def _abmid_kernel(p_ref, cs_ref, lng_ref, lnb_ref, ws_ref, bs_ref, u_ref, v_ref, yb_ref, *, fw, sw):
    gw, sg = fw // MIX_GROUPS, sw // MIX_GROUPS
    cs = cs_ref[...]
    for g in range(MIX_GROUPS):
        uv = _dot(p_ref[:, g * gw:(g + 1) * gw].astype(BF16), cs)
        u_ref[:, g * gw:(g + 1) * gw] = uv[:, :gw].astype(BF16)
        v_ref[:, g * gw:(g + 1) * gw] = uv[:, gw:].astype(BF16)
    u = _gelu(p_ref[:, fw:fw + sw].astype(F32))
    gv = _gelu(p_ref[:, fw + sw:fw + 2 * sw].astype(F32))
    mu = jnp.mean(gv, axis=-1, keepdims=True)
    dv = gv - mu
    var = jnp.mean(dv * dv, axis=-1, keepdims=True)
    vn = (dv * lax.rsqrt(var + LN_EPS) * lng_ref[...] + lnb_ref[...]).astype(BF16)
    for g in range(MIX_GROUPS):
        s = _dot(ws_ref[g].astype(BF16), vn[:, g * sg:(g + 1) * sg]) + bs_ref[:, g:g + 1]
        yb_ref[:, g * sg:(g + 1) * sg] = (u[:, g * sg:(g + 1) * sg] * s).astype(BF16)


def ab_mid(p, cs, ln_g, ln_b, ws, bs_t, *, fw, sw):
    t = p.shape[0]
    assert t % SGU_CHUNK == 0
    full = lambda a: pl.BlockSpec(a.shape, lambda i: (0,) * a.ndim)
    out = lambda w: pl.BlockSpec((SGU_CHUNK, w), lambda i: (i, 0))
    return pl.pallas_call(
        functools.partial(_abmid_kernel, fw=fw, sw=sw),
        name="ab_mid",
        out_shape=(jax.ShapeDtypeStruct((t, fw), BF16), jax.ShapeDtypeStruct((t, fw), BF16),
                   jax.ShapeDtypeStruct((t, sw), BF16)),
        grid=(t // SGU_CHUNK,),
        in_specs=[pl.BlockSpec((SGU_CHUNK, p.shape[1]), lambda i: (i, 0)),
                  full(cs), full(ln_g), full(ln_b), full(ws), full(bs_t)],
        out_specs=(out(fw), out(fw), out(sw)),
        compiler_params=_params("parallel"),
    )(p, cs, ln_g, ln_b, ws, bs_t)


def _dft_cos_sin(n):
    blk = min(n, LANES)
    nb = n // blk
    k = jnp.arange(n, dtype=jnp.int32)[:, None]
    w = 2.0 * math.pi / n
    ang_t = w * ((k * jnp.arange(blk, dtype=jnp.int32)[None, :]) % n).astype(F32)
    ang_p = w * ((k * (jnp.arange(nb, dtype=jnp.int32) * blk)[None, :]) % n).astype(F32)
    ct, st = jnp.cos(ang_t)[:, None, :], jnp.sin(ang_t)[:, None, :]
    cp, sp = jnp.cos(ang_p)[:, :, None], jnp.sin(ang_p)[:, :, None]
    return (cp * ct - sp * st).reshape(n, n), (sp * ct + cp * st).reshape(n, n)


def _dftmat_kernel(ct_ref, st_ref, cp_ref, sp_ref, c_ref, s_ref):
    ct, st = ct_ref[...], st_ref[...]
    cp, sp = cp_ref[...], sp_ref[...]
    c_ref[...] = (cp * ct - sp * st).astype(c_ref.dtype)
    s_ref[...] = (-(sp * ct + cp * st)).astype(s_ref.dtype)


def dft_position_operators(n, *, tr=256, tc=2048):
    tr, tc = min(tr, n), min(tc, n)
    assert n % tr == 0 and n % tc == 0
    w = 2.0 * math.pi / n
    l = jnp.arange(n, dtype=jnp.int32)[None, :]
    ang_t = w * ((jnp.arange(tr, dtype=jnp.int32)[:, None] * l) % n).astype(F32)
    ang_p = w * (((jnp.arange(n // tr, dtype=jnp.int32) * tr)[:, None] * l) % n).astype(F32)
    table = pl.BlockSpec((tr, tc), lambda j, i: (0, j))
    phase = pl.BlockSpec((None, 1, tc), lambda j, i: (i, 0, j))
    out = pl.BlockSpec((tr, tc), lambda j, i: (i, j))
    return pl.pallas_call(
        _dftmat_kernel,
        name="dft_operators",
        out_shape=(jax.ShapeDtypeStruct((n, n), BF16), jax.ShapeDtypeStruct((n, n), BF16)),
        grid=(n // tc, n // tr),
        in_specs=[table, table, phase, phase],
        out_specs=(out, out),
        compiler_params=_params("parallel", "parallel"),
    )(jnp.cos(ang_t), jnp.sin(ang_t), jnp.cos(ang_p)[:, None, :], jnp.sin(ang_p)[:, None, :])


def _shiftmix_kernel(cur_ref, up_ref, dn_ref, mu_ref, w1_ref, a1_ref, g1_ref,
                     xr_ref, xk_ref, xv_ref, hw_ref, ha_ref, hg_ref, *, tr, n_lat_tiles,
                     lat_tiles_per_seq, ctx_tiles_per_seq, d):
    i = pl.program_id(0)
    rows = lax.broadcasted_iota(jnp.int32, (tr, 1), 0)
    wide = {0: xr_ref, 2: xk_ref, 3: xv_ref}
    narrow = {1: w1_ref, 4: a1_ref, 5: g1_ref}

    def mix_and_project(parts):
        acc = {m: None for m in narrow}
        for lo, hi, shifted in parts:
            h = cur_ref[:, lo:hi]
            xx = shifted - h
            for m in range(6):
                x_m = (h + xx * mu_ref[m:m + 1, lo:hi]).astype(BF16)
                if m in wide:
                    wide[m][:, lo:hi] = x_m
                else:
                    part = _dot(x_m, narrow[m][lo:hi, :])
                    acc[m] = part if acc[m] is None else acc[m] + part
        hw_ref[...] = jnp.tanh(acc[1])
        ha_ref[...] = acc[4]
        hg_ref[...] = jax.nn.sigmoid(acc[5]).astype(hg_ref.dtype)

    @pl.when(i < n_lat_tiles)
    def _():
        q = d // 4
        col = rows & (GRID_W - 1)
        ib = i % lat_tiles_per_seq
        up_ok = (ib > 0).astype(F32)
        dn_ok = (ib < lat_tiles_per_seq - 1).astype(F32)
        mix_and_project([
            (0, q, jnp.where(col > 0, pltpu.roll(cur_ref[:, 0:q], 1, 0), 0.0)),
            (q, 2 * q, jnp.where(col < GRID_W - 1, pltpu.roll(cur_ref[:, q:2 * q], tr - 1, 0), 0.0)),
            (2 * q, 3 * q, jnp.concatenate(
                [up_ref[:, 2 * q:3 * q] * up_ok, cur_ref[0:tr - GRID_W, 2 * q:3 * q]], axis=0)),
            (3 * q, d, jnp.concatenate(
                [cur_ref[GRID_W:tr, 3 * q:d], dn_ref[:, 3 * q:d] * dn_ok], axis=0))])

    @pl.when(i >= n_lat_tiles)
    def _():
        hf = d // 2
        jb = (i - n_lat_tiles) % ctx_tiles_per_seq
        up_ok = (jb > 0).astype(F32)
        dn_ok = (jb < ctx_tiles_per_seq - 1).astype(F32)
        prev = jnp.where(rows == 0, up_ref[GRID_W - 1:GRID_W, 0:hf] * up_ok,
                         pltpu.roll(cur_ref[:, 0:hf], 1, 0))
        nxt = jnp.where(rows == tr - 1, dn_ref[0:1, hf:d] * dn_ok,
                        pltpu.roll(cur_ref[:, hf:d], tr - 1, 0))
        mix_and_project([(0, hf, prev), (hf, d, nxt)])


def shift_mix(h, mu8, w1, a1, g1, *, tr, n_lat, seq, ctx_len):
    t, d = h.shape
    assert t % tr == 0 and tr % GRID_W == 0 and seq % tr == 0 and ctx_len % tr == 0 and tr > GRID_W
    per = tr // GRID_W
    nhalo = t // GRID_W
    halo = lambda f: pl.BlockSpec((GRID_W, d), f)
    row = lambda w: pl.BlockSpec((tr, w), lambda i: (i, 0))
    full = lambda a: pl.BlockSpec(a.shape, lambda i: (0,) * a.ndim)
    kern = functools.partial(_shiftmix_kernel, tr=tr, n_lat_tiles=n_lat // tr,
                             lat_tiles_per_seq=seq // tr, ctx_tiles_per_seq=ctx_len // tr, d=d)
    widths = (d, d, d, w1.shape[1], a1.shape[1], g1.shape[1])
    dtypes = (BF16, BF16, BF16, F32, F32, BF16)
    return pl.pallas_call(
        kern,
        name="shift_mix",
        out_shape=tuple(jax.ShapeDtypeStruct((t, w), dt) for w, dt in zip(widths, dtypes)),
        grid=(t // tr,),
        in_specs=[row(d),
                  halo(lambda i: (jnp.maximum(i * per - 1, 0), 0)),
                  halo(lambda i: (jnp.minimum((i + 1) * per, nhalo - 1), 0)),
                  pl.BlockSpec((8, d), lambda i: (0, 0)), full(w1), full(a1), full(g1)],
        out_specs=tuple(row(w) for w in widths),
        compiler_params=_params("parallel"),
    )(h, h, h, mu8, w1, a1, g1)


def _headsum(x, bd):
    r, d = x.shape
    nt = d // LANES
    xs = jnp.concatenate([x[:, t * LANES:(t + 1) * LANES] for t in range(nt)], axis=0)
    hi = xs.astype(BF16)
    lo = (xs - hi.astype(F32)).astype(BF16)
    s = _dot(hi, bd) + _dot(lo, bd)
    return jnp.concatenate([s[t * r:(t + 1) * r] for t in range(nt)], axis=1)


def _prep_kernel(k_ref, hw_ref, ha_ref, w2_ref, a2_ref, w0_ref, a0_ref, kkp_ref, bd_ref,
                 kk_ref, lw0_ref, lw1_ref, as0_ref, as1_ref, *, lora):
    for dr, (lw_ref, as_ref) in enumerate(((lw0_ref, as0_ref), (lw1_ref, as1_ref))):
        z = _dot(hw_ref[:, dr * lora:(dr + 1) * lora].astype(BF16), w2_ref[dr]) + w0_ref[dr:dr + 1, :]
        lw_ref[...] = -DECAY_SCALE * jax.nn.sigmoid(z)
        za = _dot(ha_ref[:, dr * lora:(dr + 1) * lora].astype(BF16), a2_ref[dr]) + a0_ref[dr:dr + 1, :]
        as_ref[...] = jax.nn.sigmoid(za).astype(as_ref.dtype)
    kq = k_ref[...] * kkp_ref[...]
    kk_ref[...] = kq * lax.rsqrt(_headsum(kq * kq, bd_ref[...]) + L2_EPS)


def rwkv_prep(k, hw, ha, w2, a2, w0, a0, k_k, bd, *, tr, lora):
    t, d = k.shape
    assert t % tr == 0
    row = pl.BlockSpec((tr, d), lambda i: (i, 0))
    lrow = pl.BlockSpec((tr, 2 * lora), lambda i: (i, 0))
    full = lambda a: pl.BlockSpec(a.shape, lambda i: (0,) * a.ndim)
    return pl.pallas_call(
        functools.partial(_prep_kernel, lora=lora),
        name="rwkv_prep",
        out_shape=tuple(jax.ShapeDtypeStruct((t, d), dt) for dt in (F32, F32, F32, BF16, BF16)),
        grid=(t // tr,),
        in_specs=[row, lrow, lrow, full(w2), full(a2), full(w0), full(a0), full(k_k), full(bd)],
        out_specs=tuple(row for _ in range(5)),
        compiler_params=_params("parallel"),
    )(k, hw, ha, w2, a2, w0, a0, k_k, bd)


def _block_diag(x, mask):
    zeros = jnp.zeros((x.shape[0], LANES), BF16)
    rows = []
    for h in range(HEADS_PER_PACK):
        t, half = divmod(h, LANES // HEAD)
        tiles = [zeros] * (PACK // LANES)
        tiles[t] = jnp.where(mask[half], x[:, t * LANES:(t + 1) * LANES], 0.0).astype(BF16)
        rows.append(jnp.concatenate(tiles, axis=1))
    return jnp.concatenate(rows, axis=0)


def _scan_kernel(r_ref, k_ref, v_ref, kk_ref, as_ref, lw_ref, ka_ref, y_ref, s_ref, *, reverse, npack,
                 nsub):
    c = SCAN_CHUNK

    @pl.when(pl.program_id(2) == 0)
    def _():
        s_ref[...] = jnp.zeros_like(s_ref)

    def earlier(a, b, strict):
        if reverse:
            return (a > b) if strict else (a >= b)
        return (a < b) if strict else (a <= b)

    ti = lax.broadcasted_iota(jnp.int32, (c, c), 0)
    tj = lax.broadcasted_iota(jnp.int32, (c, c), 1)
    tri = jnp.where(earlier(tj, ti, False), 1.0, 0.0).astype(BF16)
    pt = lax.broadcasted_iota(jnp.int32, (c, PACK), 0)
    pj = lax.broadcasted_iota(jnp.int32, (c, PACK), 1) & (HEAD - 1)
    strict_m = earlier(pj, pt, True)
    incl_m = earlier(pj, pt, False)
    eye_m = pj == pt
    level_m = [((pt >> (l + 1)) == (pj >> (l + 1))) & ((pt >> l) != (pj >> l))
               for l in range(c.bit_length() - 1)]
    hshift = HEAD.bit_length() - 1
    br = lax.broadcasted_iota(jnp.int32, (PACK, PACK), 0) >> hshift
    bc = lax.broadcasted_iota(jnp.int32, (PACK, PACK), 1) >> hshift
    bmask = br == bc
    lane_head = lax.broadcasted_iota(jnp.int32, (c, LANES), 1) >> hshift
    hmask = [lane_head == half for half in range(LANES // HEAD)]
    last = 0 if reverse else c - 1

    subs = list(range(nsub - 1, -1, -1) if reverse else range(nsub))
    packs = range(npack)
    units = [(slice(sub * c, (sub + 1) * c), slice(q * PACK, (q + 1) * PACK))
             for sub in subs for q in packs]
    each = lambda f, *cols: [f(*args) for args in zip(*cols)]

    def cum_logdecay(rows, sl):
        lw = lw_ref[rows, sl]
        hi = lw.astype(BF16)
        lo = (lw - hi.astype(F32)).astype(BF16)
        return _dot(tri, hi) + _dot(tri, lo)

    cl = [cum_logdecay(rows, sl) for rows, sl in units]
    tot = [x[last:last + 1, :] for x in cl]
    e_neg = [jnp.exp(-x) for x in cl]
    asg = [as_ref[rows, sl].astype(F32) for rows, sl in units]
    kd = [k_ref[rows, sl] * (1.0 + (a - 1.0) * ka_ref[:, sl]) for (rows, sl), a in zip(units, asg)]
    bq = [kk_ref[rows, sl] * a for (rows, sl), a in zip(units, asg)]
    lhs = [jnp.concatenate([-kk_ref[rows, sl] * jnp.exp(x - lw_ref[rows, sl]),
                            r_ref[rows, sl] * jnp.exp(x)], axis=0).astype(BF16)
           for (rows, sl), x in zip(units, cl)]
    mb = each(lambda l, b, e: _dot_nt(l, _block_diag(b * e, hmask)), lhs, bq, e_neg)
    mk = each(lambda l, k, e: _dot_nt(l, _block_diag(k * e, hmask)), lhs, kd, e_neg)
    v_bd = [_block_diag(v_ref[rows, sl], hmask) for rows, sl in units]
    w_v = each(lambda m, vb: _dot(jnp.where(strict_m, m[:c], 0.0).astype(BF16), vb), mk, v_bd)
    y_v = each(lambda m, vb: _dot(jnp.where(incl_m, m[c:], 0.0).astype(BF16), vb), mk, v_bd)
    mab = [jnp.where(strict_m, m[:c], 0.0) for m in mb]
    tinv = [jnp.where(eye_m, 1.0, 0.0) + jnp.where(level_m[0], m, 0.0) for m in mab]
    for lm in level_m[1:]:
        dx = each(lambda t, m: _dot(t.astype(BF16), _block_diag(jnp.where(lm, m, 0.0), hmask)),
                  tinv, mab)
        tinv = each(lambda t, x: t + _dot(x.astype(BF16), _block_diag(t, hmask)), tinv, dx)
    mrb = [jnp.where(incl_m, m[c:], 0.0).astype(BF16) for m in mb]

    for si in range(nsub):
        ids = [si * npack + q for q in packs]
        a_s = [_dot_nt(lhs[i], s_ref[q].astype(BF16)) for q, i in zip(packs, ids)]
        u = [_dot(tinv[i].astype(BF16), _block_diag(a[:c] + w_v[i], hmask)) for a, i in zip(a_s, ids)]
        for a, x, i in zip(a_s, u, ids):
            rows, sl = units[i]
            y = a[c:] + _dot(mrb[i], _block_diag(x, hmask)) + y_v[i]
            y_ref[rows, sl] = y.astype(y_ref.dtype)
        for q, x, i in zip(packs, u, ids):
            rows, sl = units[i]
            e_end = jnp.exp(tot[i] - cl[i])
            uv_t = jnp.transpose(jnp.concatenate([x, v_ref[rows, sl]], axis=0)).astype(BF16)
            bk = jnp.concatenate([bq[i] * e_end, kd[i] * e_end], axis=0).astype(BF16)
            s_ref[q] = s_ref[q] * jnp.exp(tot[i]) + jnp.where(bmask, _dot(uv_t, bk), 0.0)


def wkv_scan(r, k, v, kk, a_sig, lw, k_a, *, reverse, nbatch, seq, ctx_len, lane_block,
             chunks_per_step=2):
    t, d = r.shape
    c = SCAN_CHUNK * chunks_per_step
    lane_block = min(lane_block, d)
    assert seq % c == 0 and ctx_len % c == 0 and d % lane_block == 0 and lane_block % PACK == 0
    n_lat_c, n_ctx_c = seq // c, ctx_len // c
    ctx0 = nbatch * n_lat_c

    def blk(b, hg, ci):
        cc = (n_ctx_c - 1 - ci) if reverse else ci
        lc = ci - n_ctx_c
        lc = (n_lat_c - 1 - lc) if reverse else lc
        return (jnp.where(ci < n_ctx_c, ctx0 + b * n_ctx_c + cc, b * n_lat_c + lc), hg)

    row = pl.BlockSpec((c, lane_block), blk)
    npack = lane_block // PACK
    return pl.pallas_call(
        functools.partial(_scan_kernel, reverse=reverse, npack=npack, nsub=chunks_per_step),
        name="wkv_scan_rev" if reverse else "wkv_scan_fwd",
        out_shape=jax.ShapeDtypeStruct((t, d), BF16),
        grid=(nbatch, d // lane_block, n_ctx_c + n_lat_c),
        in_specs=[row] * 6 + [pl.BlockSpec((1, lane_block), lambda b, hg, ci: (0, hg))],
        out_specs=row,
        scratch_shapes=[pltpu.VMEM((npack, PACK, PACK), F32)],
        compiler_params=_params("parallel", "parallel", "arbitrary"),
    )(r, k, v, kk, a_sig, lw, k_a)


def _rwkv_out_kernel(y0_ref, y1_ref, r_ref, k_ref, v_ref, as0_ref, as1_ref, g_ref,
                     ka_ref, rk_ref, lng_ref, lnb_ref, bd_ref, o_ref):
    bd = bd_ref[...]
    y = y0_ref[...].astype(F32) + y1_ref[...].astype(F32)
    inv = 1.0 / HEAD
    dy = y - _headsum(y, bd) * inv
    yn = dy * lax.rsqrt(_headsum(dy * dy, bd) * inv + GN_EPS) * lng_ref[...] + lnb_ref[...]
    a_mean = 0.5 * (as0_ref[...].astype(F32) + as1_ref[...].astype(F32))
    k_bonus = k_ref[...] * (1.0 + (a_mean - 1.0) * ka_ref[...])
    bonus = _headsum(r_ref[...] * k_bonus * rk_ref[...], bd) * v_ref[...]
    o_ref[...] = ((yn + bonus) * g_ref[...].astype(F32)).astype(o_ref.dtype)


def rwkv_out(y0, y1, r, k, v, as0, as1, g, k_a, r_k, ln_g, ln_b, bd, *, tr):
    t, d = r.shape
    assert t % tr == 0
    row = pl.BlockSpec((tr, d), lambda i: (i, 0))
    full = lambda a: pl.BlockSpec(a.shape, lambda i: (0,) * a.ndim)
    return pl.pallas_call(
        _rwkv_out_kernel,
        name="rwkv_out",
        out_shape=jax.ShapeDtypeStruct((t, d), BF16),
        grid=(t // tr,),
        in_specs=[row] * 8 + [full(k_a), full(r_k), full(ln_g), full(ln_b), full(bd)],
        out_specs=row,
        compiler_params=_params("parallel"),
    )(y0, y1, r, k, v, as0, as1, g, k_a, r_k, ln_g, ln_b, bd)


def _pad_to(a, axis, size):
    pad = [(0, 0)] * a.ndim
    pad[axis] = (0, size - a.shape[axis])
    return jnp.pad(a, pad)


def _round_up(n, m):
    return -(-n // m) * m


def kernel(x, c, ctx, c_ctx, mod_w1, mod_w2, mod_b, norm_g, ffn_w1, ffn_w3, ffn_w2, ab_w_in, ab_w_out, sgu_ln_g, sgu_ln_b, sgu_ws, sgu_bs, rwkv_mu, rwkv_w_rkv, rwkv_w_o, rwkv_w0, rwkv_w1, rwkv_w2, rwkv_a0, rwkv_a1, rwkv_a2, rwkv_g1, rwkv_g2, rwkv_k_k, rwkv_k_a, rwkv_r_k, rwkv_lnx_g, rwkv_lnx_b):
    nbatch, seq, d = x.shape
    ctx_len = ctx.shape[1]
    depth = mod_w1.shape[0]
    n_lat, n_ctx = nbatch * seq, nbatch * ctx_len
    sw = sgu_ln_g.shape[1]
    fw = ab_w_in.shape[2] - 2 * sw
    gw = fw // MIX_GROUPS
    lora = rwkv_w1.shape[3]
    tr = 256
    tm = 512
    tm_ffn = next(t for t in (1536, 1024, 768, 512, 256) if (n_lat + n_ctx) % t == 0)
    common = dict(tr=tr, seq=seq, nbatch=nbatch)

    cond = _pad_to(jax.nn.silu(jnp.concatenate([c, c_ctx[None, :]], axis=0)), 0, 8)
    mods = []
    for layer in range(depth):
        hid = matmul([cond], [(mod_w1, (layer,))], tm=8, tn=mod_w1.shape[2])
        m = matmul([hid], [(mod_w2, (layer,))], tm=8, tn=2048) + mod_b[layer]
        mods.append(_pad_to(m[:nbatch + 1].reshape(nbatch + 1, 6, d), 1, 8))
    gains = norm_g.reshape(depth * 4, 1, d)
    gain = lambda layer, j: gains[layer * 4 + j:layer * 4 + j + 1]

    quarter = d // 4
    omega = POS_BASE ** (-jnp.arange(quarter, dtype=F32) / quarter)
    ang_r = jnp.arange(seq // GRID_W, dtype=F32)[:, None] * omega
    ang_c = jnp.arange(GRID_W, dtype=F32)[:, None] * omega
    pos = jnp.concatenate(
        [jnp.repeat(jnp.concatenate([jnp.sin(ang_r), jnp.cos(ang_r)], axis=-1), GRID_W, axis=0),
         jnp.tile(jnp.concatenate([jnp.sin(ang_c), jnp.cos(ang_c)], axis=-1), (seq // GRID_W, 1))], axis=-1)

    cc, sc = _dft_cos_sin(gw)
    cs_chan = jnp.concatenate([cc, sc], axis=1).astype(BF16)
    ctx_dft = dft_position_operators(ctx_len)
    assert fw == sw
    lane = jnp.arange(LANES)
    bd = (lane[:, None] // HEAD == lane[None, :] // HEAD).astype(BF16)

    dff = ffn_w1.shape[2]
    w_in_bf, w_out_bf = ab_w_in.astype(BF16), ab_w_out.astype(BF16)
    w_rkv_bf, w_o_bf = rwkv_w_rkv.astype(BF16), rwkv_w_o.astype(BF16)
    ffn1_bf, ffn3_bf, ffn2_bf = ffn_w1.astype(BF16), ffn_w3.astype(BF16), ffn_w2.astype(BF16)
    xs, h = embed_and_norm(x.reshape(n_lat, d), ctx.reshape(n_ctx, d), pos, gain(0, 0), mods[0],
                           h_dtype=BF16, **common)

    for layer in range(depth):
        i = layer // 2
        last = layer == depth - 1
        if layer % 2 == 0:
            p = matmul([h], [(w_in_bf, (i,))], tm=tm, tn=1024, out_dtype=BF16, name="mm_ab_in")
            u, v, yb = ab_mid(p, cs_chan, sgu_ln_g[i][None, :], sgu_ln_b[i][None, :], sgu_ws[i],
                              sgu_bs[i].T, fw=fw, sw=sw)
            ya = jnp.concatenate(
                [dft_positions_two_stage(u, v, batch=b, length=seq, scale=1.0 / math.sqrt(seq * gw))
                 for b in range(nbatch)]
                + [dft_positions(u, v, *ctx_dft, nbatch=nbatch, length=ctx_len, row0=n_lat,
                                 scale=1.0 / math.sqrt(ctx_len * gw), tm=1024, tn=1024, tk=2048)], axis=0)
            o = matmul([ya, yb], [(w_out_bf, (i,), 0), (w_out_bf, (i,), 1)], tm=tm, tn=1024,
                       out_dtype=BF16, name="mm_ab_out")
        else:
            glp = _round_up(rwkv_g1.shape[2], LANES)
            xr, xk, xv, hw, ha, hg = shift_mix(
                h, _pad_to(rwkv_mu[i], 0, 8),
                jnp.concatenate([rwkv_w1[i, 0], rwkv_w1[i, 1]], axis=1).astype(BF16),
                jnp.concatenate([rwkv_a1[i, 0], rwkv_a1[i, 1]], axis=1).astype(BF16),
                _pad_to(rwkv_g1[i], 1, glp).astype(BF16), tr=128, n_lat=n_lat, seq=seq, ctx_len=ctx_len)
            r = matmul([xr], [(w_rkv_bf, (i, 0))], tm=tm, tn=1024)
            k = matmul([xk], [(w_rkv_bf, (i, 1))], tm=tm, tn=1024)
            v = matmul([xv], [(w_rkv_bf, (i, 2))], tm=tm, tn=1024)
            g = matmul([hg], [_pad_to(rwkv_g2[i], 0, glp).astype(BF16)], tm=tm, tn=1024, out_dtype=BF16)
            k_a = rwkv_k_a[i][None, :]
            kk, lw0, lw1, as0, as1 = rwkv_prep(
                k, hw, ha, rwkv_w2[i].astype(BF16), rwkv_a2[i].astype(BF16), rwkv_w0[i], rwkv_a0[i],
                rwkv_k_k[i][None, :], bd, tr=128, lora=lora)
            scan = functools.partial(wkv_scan, r, k, v, kk, nbatch=nbatch, seq=seq, ctx_len=ctx_len,
                                     lane_block=2048, chunks_per_step=1)
            y0 = scan(as0, lw0, k_a, reverse=False)
            y1 = scan(as1, lw1, k_a, reverse=True)
            ob = rwkv_out(y0, y1, r, k, v, as0, as1, g, k_a, rwkv_r_k[i].reshape(1, d),
                          rwkv_lnx_g[i][None, :], rwkv_lnx_b[i][None, :], bd, tr=128)
            o = matmul([ob], [(w_o_bf, (i,))], tm=tm, tn=1024, out_dtype=BF16, name="mm_rwkv_out")

        xs, h = gated_residual(xs, o, gain(layer, 1), mods[layer], 2, gain_next=gain(layer, 2),
                               mods_next=mods[layer], h_dtype=BF16, **common)
        hid = swiglu_up(h, (ffn1_bf, (layer,)), (ffn3_bf, (layer,)), tm=tm_ffn, tn=256)
        f = matmul([hid], [(ffn2_bf, (layer,))], tm=tm_ffn // 2, tn=1024, tk=dff // 2,
                   out_dtype=BF16, name="mm_ffn_down")
        if last:
            xs = gated_residual(xs, f, gain(layer, 3), mods[layer], 5, rows=n_lat, **common)
        else:
            xs, h = gated_residual(xs, f, gain(layer, 3), mods[layer], 5, gain_next=gain(layer + 1, 0),
                                   mods_next=mods[layer + 1],
                                   h_dtype=F32 if (layer + 1) % 2 else BF16, **common)
    return xs.reshape(nbatch, seq, d)
```

```python
import functools
import math

import jax
import jax.numpy as jnp
from jax import lax
from jax.experimental import pallas as pl
from jax.experimental.pallas import tpu as pltpu

F32 = jnp.float32
BF16 = jnp.bfloat16

GRID_W = 64
SGU_CHUNK = 128
MIX_GROUPS = 4
HEAD = 64
RMS_EPS, LN_EPS, GN_EPS, L2_EPS = 1e-6, 1e-5, 64e-5, 1e-12
POS_BASE = 10000.0
DECAY_SCALE = math.exp(-0.5)

LANES = 128
MXU_DIM = 256
SCAN_CHUNK = 64
PACK = LANES
HEADS_PER_PACK = PACK // HEAD
VMEM_LIMIT_BYTES = 48 * 1024 * 1024


def _params(*sem):
    return pltpu.CompilerParams(dimension_semantics=sem, vmem_limit_bytes=VMEM_LIMIT_BYTES)


def _fit(n, t):
    t = min(t, n)
    while n % t:
        t -= LANES
    return t


def _dot(a, b):
    return jnp.dot(a, b, preferred_element_type=F32)


def _dot_nt(a, b):
    return lax.dot_general(a, b, (((1,), (1,)), ((), ())), preferred_element_type=F32)


def _act(x, act):
    if act == "tanh":
        return jnp.tanh(x)
    if act == "sigmoid":
        return jax.nn.sigmoid(x)
    assert act is None
    return x


def _gelu(x):
    return 0.5 * x * (1.0 + jnp.tanh(math.sqrt(2.0 / math.pi) * (x + 0.044715 * (x * x * x))))


def _rms(x, g):
    return x * lax.rsqrt(jnp.mean(x * x, axis=-1, keepdims=True) + RMS_EPS) * g


def _mm_kernel(*refs, npair, nk, kaxis, act, scale):
    o_ref = refs[2 * npair]

    def product():
        part = None
        for a_ref, b_ref in zip(refs[:npair], refs[npair:2 * npair]):
            d = _dot(a_ref[...].astype(BF16), b_ref[...].astype(BF16))
            part = d if part is None else part + d
        return part

    def finish(acc):
        if scale is not None:
            acc = acc * scale
        o_ref[...] = _act(acc, act).astype(o_ref.dtype)

    if nk == 1:
        finish(product())
        return
    acc_ref = refs[2 * npair + 1]
    k = pl.program_id(kaxis)

    @pl.when(k == 0)
    def _():
        acc_ref[...] = jnp.zeros_like(acc_ref)

    acc_ref[...] += product()

    @pl.when(k == nk - 1)
    def _():
        finish(acc_ref[...])


def _weight(b):
    if isinstance(b, tuple):
        return b[0], tuple(b[1]), (b[2] if len(b) > 2 else 0)
    return b, (), 0


def _weight_spec(b, tk, tn, kj):
    arr, lead, k0 = _weight(b)
    assert arr.ndim == len(lead) + 2
    return pl.BlockSpec((None,) * len(lead) + (tk, tn), lambda *g: lead + (kj(*g)[0] + k0, kj(*g)[1]))


def matmul(a_list, b_list, *, tm, tn, tk=None, out_dtype=F32, act=None, scale=None, name="mm"):
    m, kdim = a_list[0].shape
    n = _weight(b_list[0])[0].shape[-1]
    tk = kdim if tk is None else tk
    tm, tn = min(tm, m), _fit(n, tn)
    assert m % tm == 0 and n % tn == 0 and kdim % tk == 0, (m, n, kdim, tm, tn, tk)
    nk = kdim // tk
    npair = len(a_list)
    kern = functools.partial(_mm_kernel, npair=npair, nk=nk, kaxis=2, act=act, scale=scale)
    return pl.pallas_call(
        kern,
        name=name,
        out_shape=jax.ShapeDtypeStruct((m, n), out_dtype),
        grid=(m // tm, n // tn, nk),
        in_specs=[pl.BlockSpec((tm, tk), lambda i, j, k: (i, k)) for _ in a_list]
        + [_weight_spec(b, tk, tn, lambda i, j, k: (k, j)) for b in b_list],
        out_specs=pl.BlockSpec((tm, tn), lambda i, j, k: (i, j)),
        scratch_shapes=[pltpu.VMEM((tm, tn), F32)] if nk > 1 else [],
        compiler_params=_params("parallel", "parallel", "arbitrary"),
    )(*a_list, *[_weight(b)[0] for b in b_list])


def dft_positions(u, v, cmat, smat_neg, *, nbatch, length, row0, scale, tm, tn, tk):
    width = u.shape[1]
    tm, tk, tn = min(tm, length), min(tk, length), min(tn, width)
    assert length % tm == 0 and length % tk == 0 and width % tn == 0 and row0 % tk == 0
    nk, ni = length // tk, length // tm
    kern = functools.partial(_mm_kernel, npair=2, nk=nk, kaxis=3, act=None, scale=scale)
    a_spec = pl.BlockSpec((tm, tk), lambda b, i, j, k: (i, k))
    b_spec = pl.BlockSpec((tk, tn), lambda b, i, j, k: (row0 // tk + b * nk + k, j))
    return pl.pallas_call(
        kern,
        name="dft_positions",
        out_shape=jax.ShapeDtypeStruct((nbatch * length, width), F32),
        grid=(nbatch, ni, width // tn, nk),
        in_specs=[a_spec, a_spec, b_spec, b_spec],
        out_specs=pl.BlockSpec((tm, tn), lambda b, i, j, k: (b * ni + i, j)),
        scratch_shapes=[pltpu.VMEM((tm, tn), F32)] if nk > 1 else [],
        compiler_params=_params("parallel", "parallel", "parallel", "arbitrary"),
    )(cmat, smat_neg, u, v)


def _dft_stage2_kernel(gr_ref, gi_ref, zr_ref, zi_ref, o_ref, *, scale):
    y = (_dot(gr_ref[...], zr_ref[...]) + _dot(gi_ref[...], zi_ref[...])) * scale
    o_ref[...] = y.reshape(o_ref.shape).astype(o_ref.dtype)


def dft_positions_two_stage(u, v, *, batch, length, scale):
    width = u.shape[1]
    n2 = LANES
    n1 = length // n2
    sub = 8
    assert length % n2 == 0 and n1 % sub == 0 and u.shape[0] >= (batch + 1) * length
    c2, s2 = _dft_cos_sin(n2)
    a_u = jnp.concatenate([c2, -s2], axis=0).astype(BF16)
    a_v = jnp.concatenate([-s2, -c2], axis=0).astype(BF16)
    nb = u.shape[0] // length
    view = lambda a: a[:nb * length].reshape(nb, n2, n1 * width)
    z = matmul([a_u, a_v], [(view(u), (batch,)), (view(v), (batch,))], tm=2 * n2, tn=8192,
               out_dtype=BF16, name="dft_stage1")
    z = z.reshape(2, n2 * n1, width)
    t = jnp.arange(n2 // sub, dtype=jnp.int32)[:, None, None, None]
    k1 = jnp.arange(n1, dtype=jnp.int32)[None, :, None, None]
    j = jnp.arange(sub, dtype=jnp.int32)[None, None, :, None]
    l1 = jnp.arange(n1, dtype=jnp.int32)[None, None, None, :]
    ang = (2.0 * math.pi / length) * ((l1 * (n2 * k1 + sub * t + j)) % length).astype(F32)
    eye = jnp.eye(sub, dtype=F32)[None, None, :, :, None]
    op = lambda f: (f(ang)[:, :, :, None, :] * eye).reshape(n2 // sub, sub * n1, sub * n1).astype(BF16)
    g_spec = pl.BlockSpec((None, sub * n1, sub * n1), lambda i: (i, 0, 0))
    z_spec = lambda part: pl.BlockSpec((None, sub * n1, width), lambda i: (part, i, 0))
    out = pl.pallas_call(
        functools.partial(_dft_stage2_kernel, scale=scale),
        name="dft_stage2",
        out_shape=jax.ShapeDtypeStruct((n1, n2, width), F32),
        grid=(n2 // sub,),
        in_specs=[g_spec, g_spec, z_spec(0), z_spec(1)],
        out_specs=pl.BlockSpec((n1, sub, width), lambda i: (0, i, 0)),
        compiler_params=_params("parallel"),
    )(op(jnp.cos), op(jnp.sin), z, z)
    return out.reshape(length, width)


def _swiglu_kernel(h_ref, w1_ref, w3_ref, o_ref):
    h = h_ref[...]
    a = _dot(h, w1_ref[...])
    b = _dot(h, w3_ref[...])
    o_ref[...] = (a * jax.nn.sigmoid(a) * b).astype(o_ref.dtype)


def swiglu_up(h, w1, w3, *, tm, tn):
    m, kdim = h.shape
    n = _weight(w1)[0].shape[-1]
    tm, tn = min(tm, m), min(tn, n)
    assert m % tm == 0 and n % tn == 0
    w_spec = lambda w: _weight_spec(w, kdim, tn, lambda i, j: (0, j))
    return pl.pallas_call(
        _swiglu_kernel,
        name="swiglu_up",
        out_shape=jax.ShapeDtypeStruct((m, n), BF16),
        grid=(m // tm, n // tn),
        in_specs=[pl.BlockSpec((tm, kdim), lambda i, j: (i, 0)), w_spec(w1), w_spec(w3)],
        out_specs=pl.BlockSpec((tm, tn), lambda i, j: (i, j)),
        compiler_params=_params("parallel", "parallel"),
    )(h, _weight(w1)[0], _weight(w3)[0])


def _group_map(tr, seq, nbatch):
    return lambda i: (jnp.minimum((i * tr) // seq, nbatch), 0, 0)


def _embed_kernel(x_ref, ctx_ref, pos_ref, g_ref, mod_ref, xo_ref, h_ref, *, n_lat_tiles):
    is_lat = pl.program_id(0) < n_lat_tiles
    x0 = jnp.where(is_lat, x_ref[...] + pos_ref[...], ctx_ref[...])
    xo_ref[...] = x0
    h = _rms(x0, g_ref[...]) * (1.0 + mod_ref[1:2, :]) + mod_ref[0:1, :]
    h_ref[...] = h.astype(h_ref.dtype)


def embed_and_norm(x2, ctx2, pos, gain, mods, *, tr, seq, nbatch, h_dtype):
    n_lat, d = x2.shape
    n_ctx = ctx2.shape[0]
    t = n_lat + n_ctx
    assert n_lat % tr == 0 and n_ctx % tr == 0 and seq % tr == 0
    nl, npos = n_lat // tr, seq // tr
    row = lambda f: pl.BlockSpec((tr, d), f)
    return pl.pallas_call(
        functools.partial(_embed_kernel, n_lat_tiles=nl),
        name="embed_norm",
        out_shape=(jax.ShapeDtypeStruct((t, d), F32), jax.ShapeDtypeStruct((t, d), h_dtype)),
        grid=(t // tr,),
        in_specs=[row(lambda i: (jnp.minimum(i, nl - 1), 0)),
                  row(lambda i: (jnp.maximum(i - nl, 0), 0)),
                  row(lambda i: (i % npos, 0)),
                  pl.BlockSpec((None, 1, d), lambda i: (0, 0, 0)),
                  pl.BlockSpec((None, 8, d), _group_map(tr, seq, nbatch))],
        out_specs=(row(lambda i: (i, 0)), row(lambda i: (i, 0))),
        compiler_params=_params("parallel"),
    )(x2, ctx2, pos, gain, mods)


def _resid_kernel(*refs, gate_row, with_h):
    if with_h:
        x_ref, o_ref, go_ref, modc_ref, gn_ref, modn_ref, xo_ref, h_ref = refs
    else:
        x_ref, o_ref, go_ref, modc_ref, xo_ref = refs
    xn = x_ref[...] + modc_ref[gate_row:gate_row + 1, :] * _rms(o_ref[...].astype(F32), go_ref[...])
    xo_ref[...] = xn
    if with_h:
        shift_row = 3 if gate_row == 2 else 0
        h = _rms(xn, gn_ref[...]) * (1.0 + modn_ref[shift_row + 1:shift_row + 2, :]) \
            + modn_ref[shift_row:shift_row + 1, :]
        h_ref[...] = h.astype(h_ref.dtype)


def gated_residual(x, o, gain_o, mods_cur, gate_row, *, tr, seq, nbatch, rows=None,
                   gain_next=None, mods_next=None, h_dtype=None):
    t, d = x.shape
    rows = t if rows is None else rows
    assert rows % tr == 0
    with_h = gain_next is not None
    row = pl.BlockSpec((tr, d), lambda i: (i, 0))
    gain = pl.BlockSpec((None, 1, d), lambda i: (0, 0, 0))
    mod = pl.BlockSpec((None, 8, d), _group_map(tr, seq, nbatch))
    ins, in_specs = [x, o, gain_o, mods_cur], [row, row, gain, mod]
    out_shape, out_specs = [jax.ShapeDtypeStruct((rows, d), F32)], [row]
    if with_h:
        ins += [gain_next, mods_next]
        in_specs += [gain, mod]
        out_shape.append(jax.ShapeDtypeStruct((rows, d), h_dtype))
        out_specs.append(row)
    res = pl.pallas_call(
        functools.partial(_resid_kernel, gate_row=gate_row, with_h=with_h),
        name="gated_residual",
        out_shape=tuple(out_shape),
        grid=(rows // tr,),
        in_specs=in_specs,
        out_specs=tuple(out_specs),
        compiler_params=_params("parallel"),
    )(*ins)
    return res if with_h else res[0]


def _abmid_kernel(p_ref, cs_ref, lng_ref, lnb_ref, ws_ref, bs_ref, u_ref, v_ref, yb_ref, *, fw, sw):
    gw, sg = fw // MIX_GROUPS, sw // MIX_GROUPS
    cs = cs_ref[...]
    for g in range(MIX_GROUPS):
        uv = _dot(p_ref[:, g * gw:(g + 1) * gw].astype(BF16), cs)
        u_ref[:, g * gw:(g + 1) * gw] = uv[:, :gw].astype(BF16)
        v_ref[:, g * gw:(g + 1) * gw] = uv[:, gw:].astype(BF16)
    u = _gelu(p_ref[:, fw:fw + sw].astype(F32))
    gv = _gelu(p_ref[:, fw + sw:fw + 2 * sw].astype(F32))
    mu = jnp.mean(gv, axis=-1, keepdims=True)
    dv = gv - mu
    var = jnp.mean(dv * dv, axis=-1, keepdims=True)
    vn = (dv * lax.rsqrt(var + LN_EPS) * lng_ref[...] + lnb_ref[...]).astype(BF16)
    for g in range(MIX_GROUPS):
        s = _dot(ws_ref[g].astype(BF16), vn[:, g * sg:(g + 1) * sg]) + bs_ref[:, g:g + 1]
        yb_ref[:, g * sg:(g + 1) * sg] = (u[:, g * sg:(g + 1) * sg] * s).astype(BF16)


def ab_mid(p, cs, ln_g, ln_b, ws, bs_t, *, fw, sw):
    t = p.shape[0]
    assert t % SGU_CHUNK == 0
    full = lambda a: pl.BlockSpec(a.shape, lambda i: (0,) * a.ndim)
    out = lambda w: pl.BlockSpec((SGU_CHUNK, w), lambda i: (i, 0))
    return pl.pallas_call(
        functools.partial(_abmid_kernel, fw=fw, sw=sw),
        name="ab_mid",
        out_shape=(jax.ShapeDtypeStruct((t, fw), BF16), jax.ShapeDtypeStruct((t, fw), BF16),
                   jax.ShapeDtypeStruct((t, sw), BF16)),
        grid=(t // SGU_CHUNK,),
        in_specs=[pl.BlockSpec((SGU_CHUNK, p.shape[1]), lambda i: (i, 0)),
                  full(cs), full(ln_g), full(ln_b), full(ws), full(bs_t)],
        out_specs=(out(fw), out(fw), out(sw)),
        compiler_params=_params("parallel"),
    )(p, cs, ln_g, ln_b, ws, bs_t)


def _dft_cos_sin(n):
    blk = min(n, LANES)
    nb = n // blk
    k = jnp.arange(n, dtype=jnp.int32)[:, None]
    w = 2.0 * math.pi / n
    ang_t = w * ((k * jnp.arange(blk, dtype=jnp.int32)[None, :]) % n).astype(F32)
    ang_p = w * ((k * (jnp.arange(nb, dtype=jnp.int32) * blk)[None, :]) % n).astype(F32)
    ct, st = jnp.cos(ang_t)[:, None, :], jnp.sin(ang_t)[:, None, :]
    cp, sp = jnp.cos(ang_p)[:, :, None], jnp.sin(ang_p)[:, :, None]
    return (cp * ct - sp * st).reshape(n, n), (sp * ct + cp * st).reshape(n, n)


def _dftmat_kernel(ct_ref, st_ref, cp_ref, sp_ref, c_ref, s_ref):
    ct, st = ct_ref[...], st_ref[...]
    cp, sp = cp_ref[...], sp_ref[...]
    c_ref[...] = (cp * ct - sp * st).astype(c_ref.dtype)
    s_ref[...] = (-(sp * ct + cp * st)).astype(s_ref.dtype)


def dft_position_operators(n, *, tr=256, tc=2048):
    tr, tc = min(tr, n), min(tc, n)
    assert n % tr == 0 and n % tc == 0
    w = 2.0 * math.pi / n
    l = jnp.arange(n, dtype=jnp.int32)[None, :]
    ang_t = w * ((jnp.arange(tr, dtype=jnp.int32)[:, None] * l) % n).astype(F32)
    ang_p = w * (((jnp.arange(n // tr, dtype=jnp.int32) * tr)[:, None] * l) % n).astype(F32)
    table = pl.BlockSpec((tr, tc), lambda j, i: (0, j))
    phase = pl.BlockSpec((None, 1, tc), lambda j, i: (i, 0, j))
    out = pl.BlockSpec((tr, tc), lambda j, i: (i, j))
    return pl.pallas_call(
        _dftmat_kernel,
        name="dft_operators",
        out_shape=(jax.ShapeDtypeStruct((n, n), BF16), jax.ShapeDtypeStruct((n, n), BF16)),
        grid=(n // tc, n // tr),
        in_specs=[table, table, phase, phase],
        out_specs=(out, out),
        compiler_params=_params("parallel", "parallel"),
    )(jnp.cos(ang_t), jnp.sin(ang_t), jnp.cos(ang_p)[:, None, :], jnp.sin(ang_p)[:, None, :])


def _shiftmix_kernel(cur_ref, up_ref, dn_ref, mu_ref, w1_ref, a1_ref, g1_ref,
                     xr_ref, xk_ref, xv_ref, hw_ref, ha_ref, hg_ref, *, tr, n_lat_tiles,
                     lat_tiles_per_seq, ctx_tiles_per_seq, d):
    i = pl.program_id(0)
    rows = lax.broadcasted_iota(jnp.int32, (tr, 1), 0)
    wide = {0: xr_ref, 2: xk_ref, 3: xv_ref}
    narrow = {1: w1_ref, 4: a1_ref, 5: g1_ref}

    def mix_and_project(parts):
        acc = {m: None for m in narrow}
        for lo, hi, shifted in parts:
            h = cur_ref[:, lo:hi]
            xx = shifted - h
            for m in range(6):
                x_m = (h + xx * mu_ref[m:m + 1, lo:hi]).astype(BF16)
                if m in wide:
                    wide[m][:, lo:hi] = x_m
                else:
                    part = _dot(x_m, narrow[m][lo:hi, :])
                    acc[m] = part if acc[m] is None else acc[m] + part
        hw_ref[...] = jnp.tanh(acc[1])
        ha_ref[...] = acc[4]
        hg_ref[...] = jax.nn.sigmoid(acc[5]).astype(hg_ref.dtype)

    @pl.when(i < n_lat_tiles)
    def _():
        q = d // 4
        col = rows & (GRID_W - 1)
        ib = i % lat_tiles_per_seq
        up_ok = (ib > 0).astype(F32)
        dn_ok = (ib < lat_tiles_per_seq - 1).astype(F32)
        mix_and_project([
            (0, q, jnp.where(col > 0, pltpu.roll(cur_ref[:, 0:q], 1, 0), 0.0)),
            (q, 2 * q, jnp.where(col < GRID_W - 1, pltpu.roll(cur_ref[:, q:2 * q], tr - 1, 0), 0.0)),
            (2 * q, 3 * q, jnp.concatenate(
                [up_ref[:, 2 * q:3 * q] * up_ok, cur_ref[0:tr - GRID_W, 2 * q:3 * q]], axis=0)),
            (3 * q, d, jnp.concatenate(
                [cur_ref[GRID_W:tr, 3 * q:d], dn_ref[:, 3 * q:d] * dn_ok], axis=0))])

    @pl.when(i >= n_lat_tiles)
    def _():
        hf = d // 2
        jb = (i - n_lat_tiles) % ctx_tiles_per_seq
        up_ok = (jb > 0).astype(F32)
        dn_ok = (jb < ctx_tiles_per_seq - 1).astype(F32)
        prev = jnp.where(rows == 0, up_ref[GRID_W - 1:GRID_W, 0:hf] * up_ok,
                         pltpu.roll(cur_ref[:, 0:hf], 1, 0))
        nxt = jnp.where(rows == tr - 1, dn_ref[0:1, hf:d] * dn_ok,
                        pltpu.roll(cur_ref[:, hf:d], tr - 1, 0))
        mix_and_project([(0, hf, prev), (hf, d, nxt)])


def shift_mix(h, mu8, w1, a1, g1, *, tr, n_lat, seq, ctx_len):
    t, d = h.shape
    assert t % tr == 0 and tr % GRID_W == 0 and seq % tr == 0 and ctx_len % tr == 0 and tr > GRID_W
    per = tr // GRID_W
    nhalo = t // GRID_W
    halo = lambda f: pl.BlockSpec((GRID_W, d), f)
    row = lambda w: pl.BlockSpec((tr, w), lambda i: (i, 0))
    full = lambda a: pl.BlockSpec(a.shape, lambda i: (0,) * a.ndim)
    kern = functools.partial(_shiftmix_kernel, tr=tr, n_lat_tiles=n_lat // tr,
                             lat_tiles_per_seq=seq // tr, ctx_tiles_per_seq=ctx_len // tr, d=d)
    widths = (d, d, d, w1.shape[1], a1.shape[1], g1.shape[1])
    dtypes = (BF16, BF16, BF16, F32, F32, BF16)
    return pl.pallas_call(
        kern,
        name="shift_mix",
        out_shape=tuple(jax.ShapeDtypeStruct((t, w), dt) for w, dt in zip(widths, dtypes)),
        grid=(t // tr,),
        in_specs=[row(d),
                  halo(lambda i: (jnp.maximum(i * per - 1, 0), 0)),
                  halo(lambda i: (jnp.minimum((i + 1) * per, nhalo - 1), 0)),
                  pl.BlockSpec((8, d), lambda i: (0, 0)), full(w1), full(a1), full(g1)],
        out_specs=tuple(row(w) for w in widths),
        compiler_params=_params("parallel"),
    )(h, h, h, mu8, w1, a1, g1)


def _headsum(x, bd):
    r, d = x.shape
    nt = d // LANES
    xs = jnp.concatenate([x[:, t * LANES:(t + 1) * LANES] for t in range(nt)], axis=0)
    hi = xs.astype(BF16)
    lo = (xs - hi.astype(F32)).astype(BF16)
    s = _dot(hi, bd) + _dot(lo, bd)
    return jnp.concatenate([s[t * r:(t + 1) * r] for t in range(nt)], axis=1)


def _prep_kernel(k_ref, hw_ref, ha_ref, w2_ref, a2_ref, w0_ref, a0_ref, kkp_ref, bd_ref,
                 kk_ref, lw0_ref, lw1_ref, as0_ref, as1_ref, *, lora):
    for dr, (lw_ref, as_ref) in enumerate(((lw0_ref, as0_ref), (lw1_ref, as1_ref))):
        z = _dot(hw_ref[:, dr * lora:(dr + 1) * lora].astype(BF16), w2_ref[dr]) + w0_ref[dr:dr + 1, :]
        lw_ref[...] = -DECAY_SCALE * jax.nn.sigmoid(z)
        za = _dot(ha_ref[:, dr * lora:(dr + 1) * lora].astype(BF16), a2_ref[dr]) + a0_ref[dr:dr + 1, :]
        as_ref[...] = jax.nn.sigmoid(za).astype(as_ref.dtype)
    kq = k_ref[...].astype(F32) * kkp_ref[...]
    kk_ref[...] = (kq * lax.rsqrt(_headsum(kq * kq, bd_ref[...]) + L2_EPS)).astype(kk_ref.dtype)


def rwkv_prep(k, hw, ha, w2, a2, w0, a0, k_k, bd, *, tr, lora):
    t, d = k.shape
    assert t % tr == 0
    row = pl.BlockSpec((tr, d), lambda i: (i, 0))
    lrow = pl.BlockSpec((tr, 2 * lora), lambda i: (i, 0))
    full = lambda a: pl.BlockSpec(a.shape, lambda i: (0,) * a.ndim)
    return pl.pallas_call(
        functools.partial(_prep_kernel, lora=lora),
        name="rwkv_prep",
        out_shape=tuple(jax.ShapeDtypeStruct((t, d), dt) for dt in (BF16, F32, F32, BF16, BF16)),
        grid=(t // tr,),
        in_specs=[row, lrow, lrow, full(w2), full(a2), full(w0), full(a0), full(k_k), full(bd)],
        out_specs=tuple(row for _ in range(5)),
        compiler_params=_params("parallel"),
    )(k, hw, ha, w2, a2, w0, a0, k_k, bd)


def _block_diag(x, mask):
    zeros = jnp.zeros((x.shape[0], LANES), BF16)
    rows = []
    for h in range(HEADS_PER_PACK):
        t, half = divmod(h, LANES // HEAD)
        tiles = [zeros] * (PACK // LANES)
        tiles[t] = jnp.where(mask[half], x[:, t * LANES:(t + 1) * LANES], 0.0).astype(BF16)
        rows.append(jnp.concatenate(tiles, axis=1))
    return jnp.concatenate(rows, axis=0)


def _scan_kernel(r_ref, k_ref, v_ref, kk_ref, as_ref, lw_ref, ka_ref, y_ref, s_ref, *, reverse, npack,
                 nsub):
    c = SCAN_CHUNK

    @pl.when(pl.program_id(2) == 0)
    def _():
        s_ref[...] = jnp.zeros_like(s_ref)

    def earlier(a, b, strict):
        if reverse:
            return (a > b) if strict else (a >= b)
        return (a < b) if strict else (a <= b)

    ti = lax.broadcasted_iota(jnp.int32, (c, c), 0)
    tj = lax.broadcasted_iota(jnp.int32, (c, c), 1)
    tri = jnp.where(earlier(tj, ti, False), 1.0, 0.0).astype(BF16)
    tri2 = jnp.concatenate([tri, tri], axis=1)
    pt = lax.broadcasted_iota(jnp.int32, (c, PACK), 0)
    pj = lax.broadcasted_iota(jnp.int32, (c, PACK), 1) & (HEAD - 1)
    strict_m = earlier(pj, pt, True)
    incl_m = earlier(pj, pt, False)
    pt2 = lax.broadcasted_iota(jnp.int32, (2 * c, PACK), 0)
    pj2 = lax.broadcasted_iota(jnp.int32, (2 * c, PACK), 1) & (HEAD - 1)
    both_m = earlier(pj2, pt2 & (c - 1), True) | ((pt2 >= c) & (pj2 == (pt2 & (c - 1))))
    eye_m = pj == pt
    level_m = [((pt >> (l + 1)) == (pj >> (l + 1))) & ((pt >> l) != (pj >> l))
               for l in range(c.bit_length() - 1)]
    hshift = HEAD.bit_length() - 1
    br = lax.broadcasted_iota(jnp.int32, (PACK, PACK), 0) >> hshift
    bc = lax.broadcasted_iota(jnp.int32, (PACK, PACK), 1) >> hshift
    bmask = br == bc
    lane_head = lax.broadcasted_iota(jnp.int32, (c, LANES), 1) >> hshift
    hmask = [lane_head == half for half in range(LANES // HEAD)]
    last = 0 if reverse else c - 1

    subs = list(range(nsub - 1, -1, -1) if reverse else range(nsub))
    packs = range(npack)
    units = [(slice(sub * c, (sub + 1) * c), slice(q * PACK, (q + 1) * PACK))
             for sub in subs for q in packs]
    each = lambda f, *cols: [f(*args) for args in zip(*cols)]

    def cum_logdecay(rows, sl):
        lw = lw_ref[rows, sl]
        hi = lw.astype(BF16)
        lo = (lw - hi.astype(F32)).astype(BF16)
        return _dot(tri2, jnp.concatenate([hi, lo], axis=0))

    ld = lambda ref, rows, sl: ref[rows, sl].astype(F32)
    cl = [cum_logdecay(rows, sl) for rows, sl in units]
    tot = [x[last:last + 1, :] for x in cl]
    e_neg = [jnp.exp(-x) for x in cl]
    asg = [ld(as_ref, rows, sl) for rows, sl in units]
    kd = [ld(k_ref, rows, sl) * (1.0 + (a - 1.0) * ka_ref[:, sl]) for (rows, sl), a in zip(units, asg)]
    bq = [ld(kk_ref, rows, sl) * a for (rows, sl), a in zip(units, asg)]
    lhs = [jnp.concatenate([-ld(kk_ref, rows, sl) * jnp.exp(x - lw_ref[rows, sl]),
                            ld(r_ref, rows, sl) * jnp.exp(x)], axis=0).astype(BF16)
           for (rows, sl), x in zip(units, cl)]
    mb = each(lambda l, b, e: _dot_nt(l, _block_diag(b * e, hmask)), lhs, bq, e_neg)
    mk = each(lambda l, k, e: _dot_nt(l, _block_diag(k * e, hmask)), lhs, kd, e_neg)
    v_bd = [_block_diag(ld(v_ref, rows, sl), hmask) for rows, sl in units]
    wy_v = each(lambda m, vb: _dot(jnp.where(both_m, m, 0.0).astype(BF16), vb), mk, v_bd)
    w_v = [x[:c] for x in wy_v]
    y_v = [x[c:] for x in wy_v]
    mab = [jnp.where(strict_m, m[:c], 0.0) for m in mb]
    tinv = [jnp.where(eye_m, 1.0, 0.0) + jnp.where(level_m[0], m, 0.0) for m in mab]
    for lm in level_m[1:]:
        dx = each(lambda t, m: _dot(t.astype(BF16), _block_diag(jnp.where(lm, m, 0.0), hmask)),
                  tinv, mab)
        tinv = each(lambda t, x: t + _dot(x.astype(BF16), _block_diag(t, hmask)), tinv, dx)
    mrb = [jnp.where(incl_m, m[c:], 0.0).astype(BF16) for m in mb]

    for si in range(nsub):
        ids = [si * npack + q for q in packs]
        a_s = [_dot_nt(lhs[i], s_ref[q].astype(BF16)) for q, i in zip(packs, ids)]
        u = [_dot(tinv[i].astype(BF16), _block_diag(a[:c] + w_v[i], hmask)) for a, i in zip(a_s, ids)]
        for a, x, i in zip(a_s, u, ids):
            rows, sl = units[i]
            y = a[c:] + _dot(mrb[i], _block_diag(x, hmask)) + y_v[i]
            y_ref[rows, sl] = y.astype(y_ref.dtype)
        for q, x, i in zip(packs, u, ids):
            rows, sl = units[i]
            e_end = jnp.exp(tot[i] - cl[i])
            uv_t = jnp.transpose(jnp.concatenate([x, ld(v_ref, rows, sl)], axis=0)).astype(BF16)
            bk = jnp.concatenate([bq[i] * e_end, kd[i] * e_end], axis=0).astype(BF16)
            s_ref[q] = s_ref[q] * jnp.exp(tot[i]) + jnp.where(bmask, _dot(uv_t, bk), 0.0)


def wkv_scan(r, k, v, kk, a_sig, lw, k_a, *, reverse, nbatch, seq, ctx_len, lane_block,
             chunks_per_step=2):
    t, d = r.shape
    c = SCAN_CHUNK * chunks_per_step
    lane_block = min(lane_block, d)
    assert seq % c == 0 and ctx_len % c == 0 and d % lane_block == 0 and lane_block % PACK == 0
    n_lat_c, n_ctx_c = seq // c, ctx_len // c
    ctx0 = nbatch * n_lat_c

    def blk(b, hg, ci):
        cc = (n_ctx_c - 1 - ci) if reverse else ci
        lc = ci - n_ctx_c
        lc = (n_lat_c - 1 - lc) if reverse else lc
        return (jnp.where(ci < n_ctx_c, ctx0 + b * n_ctx_c + cc, b * n_lat_c + lc), hg)

    row = pl.BlockSpec((c, lane_block), blk)
    npack = lane_block // PACK
    return pl.pallas_call(
        functools.partial(_scan_kernel, reverse=reverse, npack=npack, nsub=chunks_per_step),
        name="wkv_scan_rev" if reverse else "wkv_scan_fwd",
        out_shape=jax.ShapeDtypeStruct((t, d), BF16),
        grid=(nbatch, d // lane_block, n_ctx_c + n_lat_c),
        in_specs=[row] * 6 + [pl.BlockSpec((1, lane_block), lambda b, hg, ci: (0, hg))],
        out_specs=row,
        scratch_shapes=[pltpu.VMEM((npack, PACK, PACK), F32)],
        compiler_params=_params("parallel", "parallel", "arbitrary"),
    )(r, k, v, kk, a_sig, lw, k_a)


def _rwkv_out_kernel(y0_ref, y1_ref, r_ref, k_ref, v_ref, as0_ref, as1_ref, g_ref,
                     ka_ref, rk_ref, lng_ref, lnb_ref, bd_ref, o_ref):
    bd = bd_ref[...]
    y = y0_ref[...].astype(F32) + y1_ref[...].astype(F32)
    inv = 1.0 / HEAD
    dy = y - _headsum(y, bd) * inv
    yn = dy * lax.rsqrt(_headsum(dy * dy, bd) * inv + GN_EPS) * lng_ref[...] + lnb_ref[...]
    a_mean = 0.5 * (as0_ref[...].astype(F32) + as1_ref[...].astype(F32))
    k_bonus = k_ref[...].astype(F32) * (1.0 + (a_mean - 1.0) * ka_ref[...])
    bonus = _headsum(r_ref[...].astype(F32) * k_bonus * rk_ref[...], bd) * v_ref[...].astype(F32)
    o_ref[...] = ((yn + bonus) * g_ref[...].astype(F32)).astype(o_ref.dtype)


def rwkv_out(y0, y1, r, k, v, as0, as1, g, k_a, r_k, ln_g, ln_b, bd, *, tr):
    t, d = r.shape
    assert t % tr == 0
    row = pl.BlockSpec((tr, d), lambda i: (i, 0))
    full = lambda a: pl.BlockSpec(a.shape, lambda i: (0,) * a.ndim)
    return pl.pallas_call(
        _rwkv_out_kernel,
        name="rwkv_out",
        out_shape=jax.ShapeDtypeStruct((t, d), BF16),
        grid=(t // tr,),
        in_specs=[row] * 8 + [full(k_a), full(r_k), full(ln_g), full(ln_b), full(bd)],
        out_specs=row,
        compiler_params=_params("parallel"),
    )(y0, y1, r, k, v, as0, as1, g, k_a, r_k, ln_g, ln_b, bd)


def _pad_to(a, axis, size):
    pad = [(0, 0)] * a.ndim
    pad[axis] = (0, size - a.shape[axis])
    return jnp.pad(a, pad)


def _round_up(n, m):
    return -(-n // m) * m


def kernel(x, c, ctx, c_ctx, mod_w1, mod_w2, mod_b, norm_g, ffn_w1, ffn_w3, ffn_w2, ab_w_in, ab_w_out, sgu_ln_g, sgu_ln_b, sgu_ws, sgu_bs, rwkv_mu, rwkv_w_rkv, rwkv_w_o, rwkv_w0, rwkv_w1, rwkv_w2, rwkv_a0, rwkv_a1, rwkv_a2, rwkv_g1, rwkv_g2, rwkv_k_k, rwkv_k_a, rwkv_r_k, rwkv_lnx_g, rwkv_lnx_b):
    nbatch, seq, d = x.shape
    ctx_len = ctx.shape[1]
    depth = mod_w1.shape[0]
    n_lat, n_ctx = nbatch * seq, nbatch * ctx_len
    sw = sgu_ln_g.shape[1]
    fw = ab_w_in.shape[2] - 2 * sw
    gw = fw // MIX_GROUPS
    lora = rwkv_w1.shape[3]
    tr = 256
    tm = 512
    tm_ffn = next(t for t in (1536, 1024, 768, 512, 256) if (n_lat + n_ctx) % t == 0)
    common = dict(tr=tr, seq=seq, nbatch=nbatch)

    cond = _pad_to(jax.nn.silu(jnp.concatenate([c, c_ctx[None, :]], axis=0)), 0, 8)
    mods = []
    for layer in range(depth):
        hid = matmul([cond], [(mod_w1, (layer,))], tm=8, tn=mod_w1.shape[2])
        m = matmul([hid], [(mod_w2, (layer,))], tm=8, tn=2048) + mod_b[layer]
        mods.append(_pad_to(m[:nbatch + 1].reshape(nbatch + 1, 6, d), 1, 8))
    gains = norm_g.reshape(depth * 4, 1, d)
    gain = lambda layer, j: gains[layer * 4 + j:layer * 4 + j + 1]

    quarter = d // 4
    omega = POS_BASE ** (-jnp.arange(quarter, dtype=F32) / quarter)
    ang_r = jnp.arange(seq // GRID_W, dtype=F32)[:, None] * omega
    ang_c = jnp.arange(GRID_W, dtype=F32)[:, None] * omega
    pos = jnp.concatenate(
        [jnp.repeat(jnp.concatenate([jnp.sin(ang_r), jnp.cos(ang_r)], axis=-1), GRID_W, axis=0),
         jnp.tile(jnp.concatenate([jnp.sin(ang_c), jnp.cos(ang_c)], axis=-1), (seq // GRID_W, 1))], axis=-1)

    cc, sc = _dft_cos_sin(gw)
    cs_chan = jnp.concatenate([cc, sc], axis=1).astype(BF16)
    ctx_dft = dft_position_operators(ctx_len)
    assert fw == sw
    lane = jnp.arange(LANES)
    bd = (lane[:, None] // HEAD == lane[None, :] // HEAD).astype(BF16)

    dff = ffn_w1.shape[2]
    w_in_bf, w_out_bf = ab_w_in.astype(BF16), ab_w_out.astype(BF16)
    w_rkv_bf, w_o_bf = rwkv_w_rkv.astype(BF16), rwkv_w_o.astype(BF16)
    ffn1_bf, ffn3_bf, ffn2_bf = ffn_w1.astype(BF16), ffn_w3.astype(BF16), ffn_w2.astype(BF16)
    xs, h = embed_and_norm(x.reshape(n_lat, d), ctx.reshape(n_ctx, d), pos, gain(0, 0), mods[0],
                           h_dtype=BF16, **common)

    for layer in range(depth):
        i = layer // 2
        last = layer == depth - 1
        if layer % 2 == 0:
            p = matmul([h], [(w_in_bf, (i,))], tm=tm, tn=1024, out_dtype=BF16, name="mm_ab_in")
            u, v, yb = ab_mid(p, cs_chan, sgu_ln_g[i][None, :], sgu_ln_b[i][None, :], sgu_ws[i],
                              sgu_bs[i].T, fw=fw, sw=sw)
            ya = jnp.concatenate(
                [dft_positions_two_stage(u, v, batch=b, length=seq, scale=1.0 / math.sqrt(seq * gw))
                 for b in range(nbatch)]
                + [dft_positions(u, v, *ctx_dft, nbatch=nbatch, length=ctx_len, row0=n_lat,
                                 scale=1.0 / math.sqrt(ctx_len * gw), tm=1024, tn=1024, tk=2048)], axis=0)
            o = matmul([ya, yb], [(w_out_bf, (i,), 0), (w_out_bf, (i,), 1)], tm=tm, tn=1024,
                       out_dtype=BF16, name="mm_ab_out")
        else:
            glp = _round_up(rwkv_g1.shape[2], LANES)
            xr, xk, xv, hw, ha, hg = shift_mix(
                h, _pad_to(rwkv_mu[i], 0, 8),
                jnp.concatenate([rwkv_w1[i, 0], rwkv_w1[i, 1]], axis=1).astype(BF16),
                jnp.concatenate([rwkv_a1[i, 0], rwkv_a1[i, 1]], axis=1).astype(BF16),
                _pad_to(rwkv_g1[i], 1, glp).astype(BF16), tr=128, n_lat=n_lat, seq=seq, ctx_len=ctx_len)
            r = matmul([xr], [(w_rkv_bf, (i, 0))], tm=tm, tn=1024, out_dtype=BF16)
            k = matmul([xk], [(w_rkv_bf, (i, 1))], tm=tm, tn=1024, out_dtype=BF16)
            v = matmul([xv], [(w_rkv_bf, (i, 2))], tm=tm, tn=1024, out_dtype=BF16)
            g = matmul([hg], [_pad_to(rwkv_g2[i], 0, glp).astype(BF16)], tm=tm, tn=1024, out_dtype=BF16)
            k_a = rwkv_k_a[i][None, :]
            kk, lw0, lw1, as0, as1 = rwkv_prep(
                k, hw, ha, rwkv_w2[i].astype(BF16), rwkv_a2[i].astype(BF16), rwkv_w0[i], rwkv_a0[i],
                rwkv_k_k[i][None, :], bd, tr=128, lora=lora)
            scan = functools.partial(wkv_scan, r, k, v, kk, nbatch=nbatch, seq=seq, ctx_len=ctx_len,
                                     lane_block=2048, chunks_per_step=1)
            y0 = scan(as0, lw0, k_a, reverse=False)
            y1 = scan(as1, lw1, k_a, reverse=True)
            ob = rwkv_out(y0, y1, r, k, v, as0, as1, g, k_a, rwkv_r_k[i].reshape(1, d),
                          rwkv_lnx_g[i][None, :], rwkv_lnx_b[i][None, :], bd, tr=128)
            o = matmul([ob], [(w_o_bf, (i,))], tm=tm, tn=1024, out_dtype=BF16, name="mm_rwkv_out")

        xs, h = gated_residual(xs, o, gain(layer, 1), mods[layer], 2, gain_next=gain(layer, 2),
                               mods_next=mods[layer], h_dtype=BF16, **common)
        hid = swiglu_up(h, (ffn1_bf, (layer,)), (ffn3_bf, (layer,)), tm=tm_ffn, tn=256)
        f = matmul([hid], [(ffn2_bf, (layer,))], tm=tm_ffn // 2, tn=1024, tk=dff // 2,
                   out_dtype=BF16, name="mm_ffn_down")
        if last:
            xs = gated_residual(xs, f, gain(layer, 3), mods[layer], 5, rows=n_lat, **common)
        else:
            xs, h = gated_residual(xs, f, gain(layer, 3), mods[layer], 5, gain_next=gain(layer + 1, 0),
                                   mods_next=mods[layer + 1],
                                   h_dtype=F32 if (layer + 1) % 2 else BF16, **common)
    return xs.reshape(nbatch, seq, d)
```

```python
import functools
import math

import jax
import jax.numpy as jnp
from jax import lax
from jax.experimental import pallas as pl
from jax.experimental.pallas import tpu as pltpu

F32 = jnp.float32
BF16 = jnp.bfloat16

GRID_W = 64
SGU_CHUNK = 128
MIX_GROUPS = 4
HEAD = 64
RMS_EPS, LN_EPS, GN_EPS, L2_EPS = 1e-6, 1e-5, 64e-5, 1e-12
POS_BASE = 10000.0
DECAY_SCALE = math.exp(-0.5)

LANES = 128
MXU_DIM = 256
SCAN_CHUNK = 64
PACK = LANES
HEADS_PER_PACK = PACK // HEAD
VMEM_LIMIT_BYTES = 48 * 1024 * 1024


def _params(*sem):
    return pltpu.CompilerParams(dimension_semantics=sem, vmem_limit_bytes=VMEM_LIMIT_BYTES)


def _fit(n, t):
    t = min(t, n)
    while n % t:
        t -= LANES
    return t


def _dot(a, b):
    return jnp.dot(a, b, preferred_element_type=F32)


def _dot_nt(a, b):
    return lax.dot_general(a, b, (((1,), (1,)), ((), ())), preferred_element_type=F32)


def _act(x, act):
    if act == "tanh":
        return jnp.tanh(x)
    if act == "sigmoid":
        return jax.nn.sigmoid(x)
    assert act is None
    return x


def _gelu(x):
    return 0.5 * x * (1.0 + jnp.tanh(math.sqrt(2.0 / math.pi) * (x + 0.044715 * (x * x * x))))


def _rms(x, g):
    return x * lax.rsqrt(jnp.mean(x * x, axis=-1, keepdims=True) + RMS_EPS) * g


def _mm_kernel(*refs, npair, nk, kaxis, act, scale):
    o_ref = refs[2 * npair]

    def product():
        part = None
        for a_ref, b_ref in zip(refs[:npair], refs[npair:2 * npair]):
            d = _dot(a_ref[...].astype(BF16), b_ref[...].astype(BF16))
            part = d if part is None else part + d
        return part

    def finish(acc):
        if scale is not None:
            acc = acc * scale
        o_ref[...] = _act(acc, act).astype(o_ref.dtype)

    if nk == 1:
        finish(product())
        return
    acc_ref = refs[2 * npair + 1]
    k = pl.program_id(kaxis)

    @pl.when(k == 0)
    def _():
        acc_ref[...] = jnp.zeros_like(acc_ref)

    acc_ref[...] += product()

    @pl.when(k == nk - 1)
    def _():
        finish(acc_ref[...])


def _weight(b):
    if isinstance(b, tuple):
        return b[0], tuple(b[1]), (b[2] if len(b) > 2 else 0)
    return b, (), 0


def _weight_spec(b, tk, tn, kj):
    arr, lead, k0 = _weight(b)
    assert arr.ndim == len(lead) + 2
    return pl.BlockSpec((None,) * len(lead) + (tk, tn), lambda *g: lead + (kj(*g)[0] + k0, kj(*g)[1]))


def matmul(a_list, b_list, *, tm, tn, tk=None, out_dtype=F32, act=None, scale=None, name="mm"):
    m, kdim = a_list[0].shape
    n = _weight(b_list[0])[0].shape[-1]
    tk = kdim if tk is None else tk
    tm, tn = min(tm, m), _fit(n, tn)
    assert m % tm == 0 and n % tn == 0 and kdim % tk == 0, (m, n, kdim, tm, tn, tk)
    nk = kdim // tk
    npair = len(a_list)
    kern = functools.partial(_mm_kernel, npair=npair, nk=nk, kaxis=2, act=act, scale=scale)
    return pl.pallas_call(
        kern,
        name=name,
        out_shape=jax.ShapeDtypeStruct((m, n), out_dtype),
        grid=(m // tm, n // tn, nk),
        in_specs=[pl.BlockSpec((tm, tk), lambda i, j, k: (i, k)) for _ in a_list]
        + [_weight_spec(b, tk, tn, lambda i, j, k: (k, j)) for b in b_list],
        out_specs=pl.BlockSpec((tm, tn), lambda i, j, k: (i, j)),
        scratch_shapes=[pltpu.VMEM((tm, tn), F32)] if nk > 1 else [],
        compiler_params=_params("parallel", "parallel", "arbitrary"),
    )(*a_list, *[_weight(b)[0] for b in b_list])


def dft_positions(u, v, cmat, smat_neg, *, nbatch, length, row0, scale, tm, tn, tk):
    width = u.shape[1]
    tm, tk, tn = min(tm, length), min(tk, length), min(tn, width)
    assert length % tm == 0 and length % tk == 0 and width % tn == 0 and row0 % tk == 0
    nk, ni = length // tk, length // tm
    kern = functools.partial(_mm_kernel, npair=2, nk=nk, kaxis=3, act=None, scale=scale)
    a_spec = pl.BlockSpec((tm, tk), lambda b, i, j, k: (i, k))
    b_spec = pl.BlockSpec((tk, tn), lambda b, i, j, k: (row0 // tk + b * nk + k, j))
    return pl.pallas_call(
        kern,
        name="dft_positions",
        out_shape=jax.ShapeDtypeStruct((nbatch * length, width), F32),
        grid=(nbatch, ni, width // tn, nk),
        in_specs=[a_spec, a_spec, b_spec, b_spec],
        out_specs=pl.BlockSpec((tm, tn), lambda b, i, j, k: (b * ni + i, j)),
        scratch_shapes=[pltpu.VMEM((tm, tn), F32)] if nk > 1 else [],
        compiler_params=_params("parallel", "parallel", "parallel", "arbitrary"),
    )(cmat, smat_neg, u, v)


def _dft_stage2_kernel(gr_ref, gi_ref, zr_ref, zi_ref, o_ref, *, scale):
    y = (_dot(gr_ref[...], zr_ref[...]) + _dot(gi_ref[...], zi_ref[...])) * scale
    o_ref[...] = y.reshape(o_ref.shape).astype(o_ref.dtype)


def dft_positions_two_stage(u, v, *, batch, length, scale):
    width = u.shape[1]
    n2 = LANES
    n1 = length // n2
    sub = 8
    assert length % n2 == 0 and n1 % sub == 0 and u.shape[0] >= (batch + 1) * length
    c2, s2 = _dft_cos_sin(n2)
    a_u = jnp.concatenate([c2, -s2], axis=0).astype(BF16)
    a_v = jnp.concatenate([-s2, -c2], axis=0).astype(BF16)
    nb = u.shape[0] // length
    view = lambda a: a[:nb * length].reshape(nb, n2, n1 * width)
    z = matmul([a_u, a_v], [(view(u), (batch,)), (view(v), (batch,))], tm=2 * n2, tn=8192,
               out_dtype=BF16, name="dft_stage1")
    z = z.reshape(2, n2 * n1, width)
    t = jnp.arange(n2 // sub, dtype=jnp.int32)[:, None, None, None]
    k1 = jnp.arange(n1, dtype=jnp.int32)[None, :, None, None]
    j = jnp.arange(sub, dtype=jnp.int32)[None, None, :, None]
    l1 = jnp.arange(n1, dtype=jnp.int32)[None, None, None, :]
    ang = (2.0 * math.pi / length) * ((l1 * (n2 * k1 + sub * t + j)) % length).astype(F32)
    eye = jnp.eye(sub, dtype=F32)[None, None, :, :, None]
    op = lambda f: (f(ang)[:, :, :, None, :] * eye).reshape(n2 // sub, sub * n1, sub * n1).astype(BF16)
    g_spec = pl.BlockSpec((None, sub * n1, sub * n1), lambda i: (i, 0, 0))
    z_spec = lambda part: pl.BlockSpec((None, sub * n1, width), lambda i: (part, i, 0))
    out = pl.pallas_call(
        functools.partial(_dft_stage2_kernel, scale=scale),
        name="dft_stage2",
        out_shape=jax.ShapeDtypeStruct((n1, n2, width), F32),
        grid=(n2 // sub,),
        in_specs=[g_spec, g_spec, z_spec(0), z_spec(1)],
        out_specs=pl.BlockSpec((n1, sub, width), lambda i: (0, i, 0)),
        compiler_params=_params("parallel"),
    )(op(jnp.cos), op(jnp.sin), z, z)
    return out.reshape(length, width)


def _swiglu_kernel(h_ref, w1_ref, w3_ref, o_ref):
    h = h_ref[...]
    a = _dot(h, w1_ref[...].astype(BF16))
    b = _dot(h, w3_ref[...].astype(BF16))
    o_ref[...] = (a * jax.nn.sigmoid(a) * b).astype(o_ref.dtype)


def swiglu_up(h, w1, w3, *, tm, tn):
    m, kdim = h.shape
    n = _weight(w1)[0].shape[-1]
    tm, tn = min(tm, m), min(tn, n)
    assert m % tm == 0 and n % tn == 0
    w_spec = lambda w: _weight_spec(w, kdim, tn, lambda i, j: (0, j))
    return pl.pallas_call(
        _swiglu_kernel,
        name="swiglu_up",
        out_shape=jax.ShapeDtypeStruct((m, n), BF16),
        grid=(m // tm, n // tn),
        in_specs=[pl.BlockSpec((tm, kdim), lambda i, j: (i, 0), pipeline_mode=pl.Buffered(1)),
                  w_spec(w1), w_spec(w3)],
        out_specs=pl.BlockSpec((tm, tn), lambda i, j: (i, j)),
        compiler_params=_params("parallel", "parallel"),
    )(h, _weight(w1)[0], _weight(w3)[0])


def _group_map(tr, seq, nbatch):
    return lambda i: (jnp.minimum((i * tr) // seq, nbatch), 0, 0)


def _embed_kernel(x_ref, ctx_ref, pos_ref, g_ref, mod_ref, xo_ref, h_ref, *, n_lat_tiles):
    is_lat = pl.program_id(0) < n_lat_tiles
    x0 = jnp.where(is_lat, x_ref[...] + pos_ref[...], ctx_ref[...])
    xo_ref[...] = x0
    h = _rms(x0, g_ref[...]) * (1.0 + mod_ref[1:2, :]) + mod_ref[0:1, :]
    h_ref[...] = h.astype(h_ref.dtype)


def embed_and_norm(x2, ctx2, pos, gain, mods, *, tr, seq, nbatch, h_dtype):
    n_lat, d = x2.shape
    n_ctx = ctx2.shape[0]
    t = n_lat + n_ctx
    assert n_lat % tr == 0 and n_ctx % tr == 0 and seq % tr == 0
    nl, npos = n_lat // tr, seq // tr
    row = lambda f: pl.BlockSpec((tr, d), f)
    return pl.pallas_call(
        functools.partial(_embed_kernel, n_lat_tiles=nl),
        name="embed_norm",
        out_shape=(jax.ShapeDtypeStruct((t, d), F32), jax.ShapeDtypeStruct((t, d), h_dtype)),
        grid=(t // tr,),
        in_specs=[row(lambda i: (jnp.minimum(i, nl - 1), 0)),
                  row(lambda i: (jnp.maximum(i - nl, 0), 0)),
                  row(lambda i: (i % npos, 0)),
                  pl.BlockSpec((None, 1, d), lambda i: (0, 0, 0)),
                  pl.BlockSpec((None, 8, d), _group_map(tr, seq, nbatch))],
        out_specs=(row(lambda i: (i, 0)), row(lambda i: (i, 0))),
        compiler_params=_params("parallel"),
    )(x2, ctx2, pos, gain, mods)


def _resid_kernel(*refs, gate_row, with_h):
    if with_h:
        x_ref, o_ref, go_ref, modc_ref, gn_ref, modn_ref, xo_ref, h_ref = refs
    else:
        x_ref, o_ref, go_ref, modc_ref, xo_ref = refs
    xn = x_ref[...] + modc_ref[gate_row:gate_row + 1, :] * _rms(o_ref[...].astype(F32), go_ref[...])
    xo_ref[...] = xn
    if with_h:
        shift_row = 3 if gate_row == 2 else 0
        h = _rms(xn, gn_ref[...]) * (1.0 + modn_ref[shift_row + 1:shift_row + 2, :]) \
            + modn_ref[shift_row:shift_row + 1, :]
        h_ref[...] = h.astype(h_ref.dtype)


def gated_residual(x, o, gain_o, mods_cur, gate_row, *, tr, seq, nbatch, rows=None,
                   gain_next=None, mods_next=None, h_dtype=None):
    t, d = x.shape
    rows = t if rows is None else rows
    assert rows % tr == 0
    with_h = gain_next is not None
    row = pl.BlockSpec((tr, d), lambda i: (i, 0))
    gain = pl.BlockSpec((None, 1, d), lambda i: (0, 0, 0))
    mod = pl.BlockSpec((None, 8, d), _group_map(tr, seq, nbatch))
    ins, in_specs = [x, o, gain_o, mods_cur], [row, row, gain, mod]
    out_shape, out_specs = [jax.ShapeDtypeStruct((rows, d), F32)], [row]
    if with_h:
        ins += [gain_next, mods_next]
        in_specs += [gain, mod]
        out_shape.append(jax.ShapeDtypeStruct((rows, d), h_dtype))
        out_specs.append(row)
    res = pl.pallas_call(
        functools.partial(_resid_kernel, gate_row=gate_row, with_h=with_h),
        name="gated_residual",
        out_shape=tuple(out_shape),
        grid=(rows // tr,),
        in_specs=in_specs,
        out_specs=tuple(out_specs),
        compiler_params=_params("parallel"),
    )(*ins)
    return res if with_h else res[0]


def _abmid_kernel(p_ref, cs_ref, lng_ref, lnb_ref, ws_ref, bs_ref, u_ref, v_ref, yb_ref, *, fw, sw):
    gw, sg = fw // MIX_GROUPS, sw // MIX_GROUPS
    cs = cs_ref[...]
    for g in range(MIX_GROUPS):
        uv = _dot(p_ref[:, g * gw:(g + 1) * gw].astype(BF16), cs)
        u_ref[:, g * gw:(g + 1) * gw] = uv[:, :gw].astype(BF16)
        v_ref[:, g * gw:(g + 1) * gw] = uv[:, gw:].astype(BF16)
    u = _gelu(p_ref[:, fw:fw + sw].astype(F32))
    gv = _gelu(p_ref[:, fw + sw:fw + 2 * sw].astype(F32))
    mu = jnp.mean(gv, axis=-1, keepdims=True)
    dv = gv - mu
    var = jnp.mean(dv * dv, axis=-1, keepdims=True)
    vn = (dv * lax.rsqrt(var + LN_EPS) * lng_ref[...] + lnb_ref[...]).astype(BF16)
    for g in range(MIX_GROUPS):
        s = _dot(ws_ref[g].astype(BF16), vn[:, g * sg:(g + 1) * sg]) + bs_ref[:, g:g + 1]
        yb_ref[:, g * sg:(g + 1) * sg] = (u[:, g * sg:(g + 1) * sg] * s).astype(BF16)


def ab_mid(p, cs, ln_g, ln_b, ws, bs_t, *, fw, sw):
    t = p.shape[0]
    assert t % SGU_CHUNK == 0
    full = lambda a: pl.BlockSpec(a.shape, lambda i: (0,) * a.ndim)
    out = lambda w: pl.BlockSpec((SGU_CHUNK, w), lambda i: (i, 0))
    return pl.pallas_call(
        functools.partial(_abmid_kernel, fw=fw, sw=sw),
        name="ab_mid",
        out_shape=(jax.ShapeDtypeStruct((t, fw), BF16), jax.ShapeDtypeStruct((t, fw), BF16),
                   jax.ShapeDtypeStruct((t, sw), BF16)),
        grid=(t // SGU_CHUNK,),
        in_specs=[pl.BlockSpec((SGU_CHUNK, p.shape[1]), lambda i: (i, 0)),
                  full(cs), full(ln_g), full(ln_b), full(ws), full(bs_t)],
        out_specs=(out(fw), out(fw), out(sw)),
        compiler_params=_params("parallel"),
    )(p, cs, ln_g, ln_b, ws, bs_t)


def _dft_cos_sin(n):
    blk = min(n, LANES)
    nb = n // blk
    k = jnp.arange(n, dtype=jnp.int32)[:, None]
    w = 2.0 * math.pi / n
    ang_t = w * ((k * jnp.arange(blk, dtype=jnp.int32)[None, :]) % n).astype(F32)
    ang_p = w * ((k * (jnp.arange(nb, dtype=jnp.int32) * blk)[None, :]) % n).astype(F32)
    ct, st = jnp.cos(ang_t)[:, None, :], jnp.sin(ang_t)[:, None, :]
    cp, sp = jnp.cos(ang_p)[:, :, None], jnp.sin(ang_p)[:, :, None]
    return (cp * ct - sp * st).reshape(n, n), (sp * ct + cp * st).reshape(n, n)


def _dftmat_kernel(ct_ref, st_ref, cp_ref, sp_ref, c_ref, s_ref):
    ct, st = ct_ref[...], st_ref[...]
    cp, sp = cp_ref[...], sp_ref[...]
    c_ref[...] = (cp * ct - sp * st).astype(c_ref.dtype)
    s_ref[...] = (-(sp * ct + cp * st)).astype(s_ref.dtype)


def dft_position_operators(n, *, tr=256, tc=2048):
    tr, tc = min(tr, n), min(tc, n)
    assert n % tr == 0 and n % tc == 0
    w = 2.0 * math.pi / n
    l = jnp.arange(n, dtype=jnp.int32)[None, :]
    ang_t = w * ((jnp.arange(tr, dtype=jnp.int32)[:, None] * l) % n).astype(F32)
    ang_p = w * (((jnp.arange(n // tr, dtype=jnp.int32) * tr)[:, None] * l) % n).astype(F32)
    table = pl.BlockSpec((tr, tc), lambda j, i: (0, j))
    phase = pl.BlockSpec((None, 1, tc), lambda j, i: (i, 0, j))
    out = pl.BlockSpec((tr, tc), lambda j, i: (i, j))
    return pl.pallas_call(
        _dftmat_kernel,
        name="dft_operators",
        out_shape=(jax.ShapeDtypeStruct((n, n), BF16), jax.ShapeDtypeStruct((n, n), BF16)),
        grid=(n // tc, n // tr),
        in_specs=[table, table, phase, phase],
        out_specs=(out, out),
        compiler_params=_params("parallel", "parallel"),
    )(jnp.cos(ang_t), jnp.sin(ang_t), jnp.cos(ang_p)[:, None, :], jnp.sin(ang_p)[:, None, :])


def _shiftmix_kernel(cur_ref, up_ref, dn_ref, mu_ref, w1_ref, a1_ref, g1_ref,
                     xr_ref, xk_ref, xv_ref, hw_ref, ha_ref, hg_ref, *, tr, n_lat_tiles,
                     lat_tiles_per_seq, ctx_tiles_per_seq, d):
    i = pl.program_id(0)
    rows = lax.broadcasted_iota(jnp.int32, (tr, 1), 0)
    wide = {0: xr_ref, 2: xk_ref, 3: xv_ref}
    narrow = {1: w1_ref, 4: a1_ref, 5: g1_ref}

    def mix_and_project(parts):
        acc = {m: None for m in narrow}
        for lo, hi, shifted in parts:
            h = cur_ref[:, lo:hi]
            xx = shifted - h
            for m in range(6):
                x_m = (h + xx * mu_ref[m:m + 1, lo:hi]).astype(BF16)
                if m in wide:
                    wide[m][:, lo:hi] = x_m
                else:
                    part = _dot(x_m, narrow[m][lo:hi, :])
                    acc[m] = part if acc[m] is None else acc[m] + part
        hw_ref[...] = jnp.tanh(acc[1])
        ha_ref[...] = acc[4]
        hg_ref[...] = jax.nn.sigmoid(acc[5]).astype(hg_ref.dtype)

    @pl.when(i < n_lat_tiles)
    def _():
        q = d // 4
        col = rows & (GRID_W - 1)
        ib = i % lat_tiles_per_seq
        up_ok = (ib > 0).astype(F32)
        dn_ok = (ib < lat_tiles_per_seq - 1).astype(F32)
        mix_and_project([
            (0, q, jnp.where(col > 0, pltpu.roll(cur_ref[:, 0:q], 1, 0), 0.0)),
            (q, 2 * q, jnp.where(col < GRID_W - 1, pltpu.roll(cur_ref[:, q:2 * q], tr - 1, 0), 0.0)),
            (2 * q, 3 * q, jnp.concatenate(
                [up_ref[:, 2 * q:3 * q] * up_ok, cur_ref[0:tr - GRID_W, 2 * q:3 * q]], axis=0)),
            (3 * q, d, jnp.concatenate(
                [cur_ref[GRID_W:tr, 3 * q:d], dn_ref[:, 3 * q:d] * dn_ok], axis=0))])

    @pl.when(i >= n_lat_tiles)
    def _():
        hf = d // 2
        jb = (i - n_lat_tiles) % ctx_tiles_per_seq
        up_ok = (jb > 0).astype(F32)
        dn_ok = (jb < ctx_tiles_per_seq - 1).astype(F32)
        prev = jnp.where(rows == 0, up_ref[GRID_W - 1:GRID_W, 0:hf] * up_ok,
                         pltpu.roll(cur_ref[:, 0:hf], 1, 0))
        nxt = jnp.where(rows == tr - 1, dn_ref[0:1, hf:d] * dn_ok,
                        pltpu.roll(cur_ref[:, hf:d], tr - 1, 0))
        mix_and_project([(0, hf, prev), (hf, d, nxt)])


def shift_mix(h, mu8, w1, a1, g1, *, tr, n_lat, seq, ctx_len):
    t, d = h.shape
    assert t % tr == 0 and tr % GRID_W == 0 and seq % tr == 0 and ctx_len % tr == 0 and tr > GRID_W
    per = tr // GRID_W
    nhalo = t // GRID_W
    halo = lambda f: pl.BlockSpec((GRID_W, d), f)
    row = lambda w: pl.BlockSpec((tr, w), lambda i: (i, 0))
    full = lambda a: pl.BlockSpec(a.shape, lambda i: (0,) * a.ndim)
    kern = functools.partial(_shiftmix_kernel, tr=tr, n_lat_tiles=n_lat // tr,
                             lat_tiles_per_seq=seq // tr, ctx_tiles_per_seq=ctx_len // tr, d=d)
    widths = (d, d, d, w1.shape[1], a1.shape[1], g1.shape[1])
    dtypes = (BF16, BF16, BF16, F32, F32, BF16)
    return pl.pallas_call(
        kern,
        name="shift_mix",
        out_shape=tuple(jax.ShapeDtypeStruct((t, w), dt) for w, dt in zip(widths, dtypes)),
        grid=(t // tr,),
        in_specs=[row(d),
                  halo(lambda i: (jnp.maximum(i * per - 1, 0), 0)),
                  halo(lambda i: (jnp.minimum((i + 1) * per, nhalo - 1), 0)),
                  pl.BlockSpec((8, d), lambda i: (0, 0)), full(w1), full(a1), full(g1)],
        out_specs=tuple(row(w) for w in widths),
        compiler_params=_params("parallel"),
    )(h, h, h, mu8, w1, a1, g1)


def _headsum(x, bd):
    r, d = x.shape
    nt = d // LANES
    xs = jnp.concatenate([x[:, t * LANES:(t + 1) * LANES] for t in range(nt)], axis=0)
    hi = xs.astype(BF16)
    lo = (xs - hi.astype(F32)).astype(BF16)
    s = _dot(hi, bd) + _dot(lo, bd)
    return jnp.concatenate([s[t * r:(t + 1) * r] for t in range(nt)], axis=1)


def _prep_kernel(k_ref, hw_ref, ha_ref, w2_ref, a2_ref, w0_ref, a0_ref, kkp_ref, bd_ref,
                 kk_ref, lw0_ref, lw1_ref, as0_ref, as1_ref, *, lora):
    for dr, (lw_ref, as_ref) in enumerate(((lw0_ref, as0_ref), (lw1_ref, as1_ref))):
        z = _dot(hw_ref[:, dr * lora:(dr + 1) * lora].astype(BF16), w2_ref[dr]) + w0_ref[dr:dr + 1, :]
        lw_ref[...] = -DECAY_SCALE * jax.nn.sigmoid(z)
        za = _dot(ha_ref[:, dr * lora:(dr + 1) * lora].astype(BF16), a2_ref[dr]) + a0_ref[dr:dr + 1, :]
        as_ref[...] = jax.nn.sigmoid(za).astype(as_ref.dtype)
    kq = k_ref[...].astype(F32) * kkp_ref[...]
    kk_ref[...] = (kq * lax.rsqrt(_headsum(kq * kq, bd_ref[...]) + L2_EPS)).astype(kk_ref.dtype)


def rwkv_prep(k, hw, ha, w2, a2, w0, a0, k_k, bd, *, tr, lora):
    t, d = k.shape
    assert t % tr == 0
    row = pl.BlockSpec((tr, d), lambda i: (i, 0))
    lrow = pl.BlockSpec((tr, 2 * lora), lambda i: (i, 0))
    full = lambda a: pl.BlockSpec(a.shape, lambda i: (0,) * a.ndim)
    return pl.pallas_call(
        functools.partial(_prep_kernel, lora=lora),
        name="rwkv_prep",
        out_shape=tuple(jax.ShapeDtypeStruct((t, d), dt) for dt in (BF16, F32, F32, BF16, BF16)),
        grid=(t // tr,),
        in_specs=[row, lrow, lrow, full(w2), full(a2), full(w0), full(a0), full(k_k), full(bd)],
        out_specs=tuple(row for _ in range(5)),
        compiler_params=_params("parallel"),
    )(k, hw, ha, w2, a2, w0, a0, k_k, bd)


def _block_diag(x, mask):
    zeros = jnp.zeros((x.shape[0], LANES), BF16)
    rows = []
    for h in range(HEADS_PER_PACK):
        t, half = divmod(h, LANES // HEAD)
        tiles = [zeros] * (PACK // LANES)
        tiles[t] = jnp.where(mask[half], x[:, t * LANES:(t + 1) * LANES], 0.0).astype(BF16)
        rows.append(jnp.concatenate(tiles, axis=1))
    return jnp.concatenate(rows, axis=0)


def _scan_kernel(r_ref, k_ref, v_ref, kk_ref, as_ref, lw_ref, ka_ref, y_ref, s_ref, *, reverse, npack,
                 nsub):
    c = SCAN_CHUNK

    @pl.when(pl.program_id(2) == 0)
    def _():
        s_ref[...] = jnp.zeros_like(s_ref)

    def earlier(a, b, strict):
        if reverse:
            return (a > b) if strict else (a >= b)
        return (a < b) if strict else (a <= b)

    ti = lax.broadcasted_iota(jnp.int32, (c, c), 0)
    tj = lax.broadcasted_iota(jnp.int32, (c, c), 1)
    tri = jnp.where(earlier(tj, ti, False), 1.0, 0.0).astype(BF16)
    tri2 = jnp.concatenate([tri, tri], axis=1)
    pt = lax.broadcasted_iota(jnp.int32, (c, PACK), 0)
    pj = lax.broadcasted_iota(jnp.int32, (c, PACK), 1) & (HEAD - 1)
    strict_m = earlier(pj, pt, True)
    incl_m = earlier(pj, pt, False)
    pt2 = lax.broadcasted_iota(jnp.int32, (2 * c, PACK), 0)
    pj2 = lax.broadcasted_iota(jnp.int32, (2 * c, PACK), 1) & (HEAD - 1)
    both_m = earlier(pj2, pt2 & (c - 1), True) | ((pt2 >= c) & (pj2 == (pt2 & (c - 1))))
    eye_m = pj == pt
    level_m = [((pt >> (l + 1)) == (pj >> (l + 1))) & ((pt >> l) != (pj >> l))
               for l in range(c.bit_length() - 1)]
    hshift = HEAD.bit_length() - 1
    br = lax.broadcasted_iota(jnp.int32, (PACK, PACK), 0) >> hshift
    bc = lax.broadcasted_iota(jnp.int32, (PACK, PACK), 1) >> hshift
    bmask = br == bc
    lane_head = lax.broadcasted_iota(jnp.int32, (c, LANES), 1) >> hshift
    hmask = [lane_head == half for half in range(LANES // HEAD)]
    last = 0 if reverse else c - 1

    subs = list(range(nsub - 1, -1, -1) if reverse else range(nsub))
    packs = range(npack)
    units = [(slice(sub * c, (sub + 1) * c), slice(q * PACK, (q + 1) * PACK))
             for sub in subs for q in packs]
    each = lambda f, *cols: [f(*args) for args in zip(*cols)]

    def cum_logdecay(rows, sl):
        lw = lw_ref[rows, sl]
        hi = lw.astype(BF16)
        lo = (lw - hi.astype(F32)).astype(BF16)
        return _dot(tri2, jnp.concatenate([hi, lo], axis=0))

    ld = lambda ref, rows, sl: ref[rows, sl].astype(F32)
    cl = [cum_logdecay(rows, sl) for rows, sl in units]
    tot = [x[last:last + 1, :] for x in cl]
    e_neg = [jnp.exp(-x) for x in cl]
    asg = [ld(as_ref, rows, sl) for rows, sl in units]
    kd = [ld(k_ref, rows, sl) * (1.0 + (a - 1.0) * ka_ref[:, sl]) for (rows, sl), a in zip(units, asg)]
    bq = [ld(kk_ref, rows, sl) * a for (rows, sl), a in zip(units, asg)]
    lhs = [jnp.concatenate([-ld(kk_ref, rows, sl) * jnp.exp(x - lw_ref[rows, sl]),
                            ld(r_ref, rows, sl) * jnp.exp(x)], axis=0).astype(BF16)
           for (rows, sl), x in zip(units, cl)]
    mb = each(lambda l, b, e: _dot_nt(l, _block_diag(b * e, hmask)), lhs, bq, e_neg)
    mk = each(lambda l, k, e: _dot_nt(l, _block_diag(k * e, hmask)), lhs, kd, e_neg)
    v_bd = [_block_diag(ld(v_ref, rows, sl), hmask) for rows, sl in units]
    wy_v = each(lambda m, vb: _dot(jnp.where(both_m, m, 0.0).astype(BF16), vb), mk, v_bd)
    w_v = [x[:c] for x in wy_v]
    y_v = [x[c:] for x in wy_v]
    mab = [jnp.where(strict_m, m[:c], 0.0) for m in mb]
    tinv = [jnp.where(eye_m, 1.0, 0.0) + jnp.where(level_m[0], m, 0.0) for m in mab]
    for lm in level_m[1:]:
        dx = each(lambda t, m: _dot(t.astype(BF16), _block_diag(jnp.where(lm, m, 0.0), hmask)),
                  tinv, mab)
        tinv = each(lambda t, x: t + _dot(x.astype(BF16), _block_diag(t, hmask)), tinv, dx)
    mrb = [jnp.where(incl_m, m[c:], 0.0).astype(BF16) for m in mb]

    for si in range(nsub):
        ids = [si * npack + q for q in packs]
        a_s = [_dot_nt(lhs[i], s_ref[q].astype(BF16)) for q, i in zip(packs, ids)]
        u = [_dot(tinv[i].astype(BF16), _block_diag(a[:c] + w_v[i], hmask)) for a, i in zip(a_s, ids)]
        for a, x, i in zip(a_s, u, ids):
            rows, sl = units[i]
            y = a[c:] + _dot(mrb[i], _block_diag(x, hmask)) + y_v[i]
            y_ref[rows, sl] = y.astype(y_ref.dtype)
        for q, x, i in zip(packs, u, ids):
            rows, sl = units[i]
            e_end = jnp.exp(tot[i] - cl[i])
            uv_t = jnp.transpose(jnp.concatenate([x, ld(v_ref, rows, sl)], axis=0)).astype(BF16)
            bk = jnp.concatenate([bq[i] * e_end, kd[i] * e_end], axis=0).astype(BF16)
            s_ref[q] = s_ref[q] * jnp.exp(tot[i]) + jnp.where(bmask, _dot(uv_t, bk), 0.0)


def wkv_scan(r, k, v, kk, a_sig, lw, k_a, *, reverse, nbatch, seq, ctx_len, lane_block,
             chunks_per_step=2):
    t, d = r.shape
    c = SCAN_CHUNK * chunks_per_step
    lane_block = min(lane_block, d)
    assert seq % c == 0 and ctx_len % c == 0 and d % lane_block == 0 and lane_block % PACK == 0
    n_lat_c, n_ctx_c = seq // c, ctx_len // c
    ctx0 = nbatch * n_lat_c

    def blk(b, hg, ci):
        cc = (n_ctx_c - 1 - ci) if reverse else ci
        lc = ci - n_ctx_c
        lc = (n_lat_c - 1 - lc) if reverse else lc
        return (jnp.where(ci < n_ctx_c, ctx0 + b * n_ctx_c + cc, b * n_lat_c + lc), hg)

    row = pl.BlockSpec((c, lane_block), blk)
    npack = lane_block // PACK
    return pl.pallas_call(
        functools.partial(_scan_kernel, reverse=reverse, npack=npack, nsub=chunks_per_step),
        name="wkv_scan_rev" if reverse else "wkv_scan_fwd",
        out_shape=jax.ShapeDtypeStruct((t, d), BF16),
        grid=(nbatch, d // lane_block, n_ctx_c + n_lat_c),
        in_specs=[row] * 6 + [pl.BlockSpec((1, lane_block), lambda b, hg, ci: (0, hg))],
        out_specs=row,
        scratch_shapes=[pltpu.VMEM((npack, PACK, PACK), F32)],
        compiler_params=_params("parallel", "parallel", "arbitrary"),
    )(r, k, v, kk, a_sig, lw, k_a)


def _rwkv_out_kernel(y0_ref, y1_ref, r_ref, k_ref, v_ref, as0_ref, as1_ref, g_ref,
                     ka_ref, rk_ref, lng_ref, lnb_ref, bd_ref, o_ref):
    bd = bd_ref[...]
    y = y0_ref[...].astype(F32) + y1_ref[...].astype(F32)
    inv = 1.0 / HEAD
    dy = y - _headsum(y, bd) * inv
    yn = dy * lax.rsqrt(_headsum(dy * dy, bd) * inv + GN_EPS) * lng_ref[...] + lnb_ref[...]
    a_mean = 0.5 * (as0_ref[...].astype(F32) + as1_ref[...].astype(F32))
    k_bonus = k_ref[...].astype(F32) * (1.0 + (a_mean - 1.0) * ka_ref[...])
    bonus = _headsum(r_ref[...].astype(F32) * k_bonus * rk_ref[...], bd) * v_ref[...].astype(F32)
    o_ref[...] = ((yn + bonus) * g_ref[...].astype(F32)).astype(o_ref.dtype)


def rwkv_out(y0, y1, r, k, v, as0, as1, g, k_a, r_k, ln_g, ln_b, bd, *, tr):
    t, d = r.shape
    assert t % tr == 0
    row = pl.BlockSpec((tr, d), lambda i: (i, 0))
    full = lambda a: pl.BlockSpec(a.shape, lambda i: (0,) * a.ndim)
    return pl.pallas_call(
        _rwkv_out_kernel,
        name="rwkv_out",
        out_shape=jax.ShapeDtypeStruct((t, d), BF16),
        grid=(t // tr,),
        in_specs=[row] * 8 + [full(k_a), full(r_k), full(ln_g), full(ln_b), full(bd)],
        out_specs=row,
        compiler_params=_params("parallel"),
    )(y0, y1, r, k, v, as0, as1, g, k_a, r_k, ln_g, ln_b, bd)


def _pad_to(a, axis, size):
    pad = [(0, 0)] * a.ndim
    pad[axis] = (0, size - a.shape[axis])
    return jnp.pad(a, pad)


def _round_up(n, m):
    return -(-n // m) * m


def kernel(x, c, ctx, c_ctx, mod_w1, mod_w2, mod_b, norm_g, ffn_w1, ffn_w3, ffn_w2, ab_w_in, ab_w_out, sgu_ln_g, sgu_ln_b, sgu_ws, sgu_bs, rwkv_mu, rwkv_w_rkv, rwkv_w_o, rwkv_w0, rwkv_w1, rwkv_w2, rwkv_a0, rwkv_a1, rwkv_a2, rwkv_g1, rwkv_g2, rwkv_k_k, rwkv_k_a, rwkv_r_k, rwkv_lnx_g, rwkv_lnx_b):
    nbatch, seq, d = x.shape
    ctx_len = ctx.shape[1]
    depth = mod_w1.shape[0]
    n_lat, n_ctx = nbatch * seq, nbatch * ctx_len
    sw = sgu_ln_g.shape[1]
    fw = ab_w_in.shape[2] - 2 * sw
    gw = fw // MIX_GROUPS
    lora = rwkv_w1.shape[3]
    tr = 256
    tm = 512
    tm_ffn = next(t for t in (1536, 1024, 768, 512, 256) if (n_lat + n_ctx) % t == 0)
    common = dict(tr=tr, seq=seq, nbatch=nbatch)

    cond = _pad_to(jax.nn.silu(jnp.concatenate([c, c_ctx[None, :]], axis=0)), 0, 8)
    mods = []
    for layer in range(depth):
        hid = matmul([cond], [(mod_w1, (layer,))], tm=8, tn=mod_w1.shape[2])
        m = matmul([hid], [(mod_w2, (layer,))], tm=8, tn=2048) + mod_b[layer]
        mods.append(_pad_to(m[:nbatch + 1].reshape(nbatch + 1, 6, d), 1, 8))
    gains = norm_g.reshape(depth * 4, 1, d)
    gain = lambda layer, j: gains[layer * 4 + j:layer * 4 + j + 1]

    quarter = d // 4
    omega = POS_BASE ** (-jnp.arange(quarter, dtype=F32) / quarter)
    ang_r = jnp.arange(seq // GRID_W, dtype=F32)[:, None] * omega
    ang_c = jnp.arange(GRID_W, dtype=F32)[:, None] * omega
    pos = jnp.concatenate(
        [jnp.repeat(jnp.concatenate([jnp.sin(ang_r), jnp.cos(ang_r)], axis=-1), GRID_W, axis=0),
         jnp.tile(jnp.concatenate([jnp.sin(ang_c), jnp.cos(ang_c)], axis=-1), (seq // GRID_W, 1))], axis=-1)

    cc, sc = _dft_cos_sin(gw)
    cs_chan = jnp.concatenate([cc, sc], axis=1).astype(BF16)
    ctx_dft = dft_position_operators(ctx_len)
    assert fw == sw
    lane = jnp.arange(LANES)
    bd = (lane[:, None] // HEAD == lane[None, :] // HEAD).astype(BF16)

    dff = ffn_w1.shape[2]
    w_in_bf, w_out_bf = ab_w_in.astype(BF16), ab_w_out.astype(BF16)
    w_rkv_bf, w_o_bf = rwkv_w_rkv.astype(BF16), rwkv_w_o.astype(BF16)
    ffn2_bf = ffn_w2.astype(BF16)
    xs, h = embed_and_norm(x.reshape(n_lat, d), ctx.reshape(n_ctx, d), pos, gain(0, 0), mods[0],
                           h_dtype=BF16, **common)

    for layer in range(depth):
        i = layer // 2
        last = layer == depth - 1
        if layer % 2 == 0:
            p = matmul([h], [(w_in_bf, (i,))], tm=tm, tn=1024, out_dtype=BF16, name="mm_ab_in")
            u, v, yb = ab_mid(p, cs_chan, sgu_ln_g[i][None, :], sgu_ln_b[i][None, :], sgu_ws[i],
                              sgu_bs[i].T, fw=fw, sw=sw)
            ya = jnp.concatenate(
                [dft_positions_two_stage(u, v, batch=b, length=seq, scale=1.0 / math.sqrt(seq * gw))
                 for b in range(nbatch)]
                + [dft_positions(u, v, *ctx_dft, nbatch=nbatch, length=ctx_len, row0=n_lat,
                                 scale=1.0 / math.sqrt(ctx_len * gw), tm=1024, tn=1024, tk=2048)], axis=0)
            o = matmul([ya, yb], [(w_out_bf, (i,), 0), (w_out_bf, (i,), 1)], tm=tm, tn=1024,
                       out_dtype=BF16, name="mm_ab_out")
        else:
            glp = _round_up(rwkv_g1.shape[2], LANES)
            xr, xk, xv, hw, ha, hg = shift_mix(
                h, _pad_to(rwkv_mu[i], 0, 8),
                jnp.concatenate([rwkv_w1[i, 0], rwkv_w1[i, 1]], axis=1).astype(BF16),
                jnp.concatenate([rwkv_a1[i, 0], rwkv_a1[i, 1]], axis=1).astype(BF16),
                _pad_to(rwkv_g1[i], 1, glp).astype(BF16), tr=128, n_lat=n_lat, seq=seq, ctx_len=ctx_len)
            r = matmul([xr], [(w_rkv_bf, (i, 0))], tm=tm, tn=1024, out_dtype=BF16)
            k = matmul([xk], [(w_rkv_bf, (i, 1))], tm=tm, tn=1024, out_dtype=BF16)
            v = matmul([xv], [(w_rkv_bf, (i, 2))], tm=tm, tn=1024, out_dtype=BF16)
            g = matmul([hg], [_pad_to(rwkv_g2[i], 0, glp).astype(BF16)], tm=tm, tn=1024, out_dtype=BF16)
            k_a = rwkv_k_a[i][None, :]
            kk, lw0, lw1, as0, as1 = rwkv_prep(
                k, hw, ha, rwkv_w2[i].astype(BF16), rwkv_a2[i].astype(BF16), rwkv_w0[i], rwkv_a0[i],
                rwkv_k_k[i][None, :], bd, tr=128, lora=lora)
            scan = functools.partial(wkv_scan, r, k, v, kk, nbatch=nbatch, seq=seq, ctx_len=ctx_len,
                                     lane_block=2048, chunks_per_step=1)
            y0 = scan(as0, lw0, k_a, reverse=False)
            y1 = scan(as1, lw1, k_a, reverse=True)
            ob = rwkv_out(y0, y1, r, k, v, as0, as1, g, k_a, rwkv_r_k[i].reshape(1, d),
                          rwkv_lnx_g[i][None, :], rwkv_lnx_b[i][None, :], bd, tr=128)
            o = matmul([ob], [(w_o_bf, (i,))], tm=tm, tn=1024, out_dtype=BF16, name="mm_rwkv_out")

        xs, h = gated_residual(xs, o, gain(layer, 1), mods[layer], 2, gain_next=gain(layer, 2),
                               mods_next=mods[layer], h_dtype=BF16, **common)
        hid = swiglu_up(h, (ffn_w1, (layer,)), (ffn_w3, (layer,)), tm=tm_ffn, tn=256)
        f = matmul([hid], [(ffn2_bf, (layer,))], tm=tm_ffn // 2, tn=1024, tk=dff // 2,
                   out_dtype=BF16, name="mm_ffn_down")
        if last:
            xs = gated_residual(xs, f, gain(layer, 3), mods[layer], 5, rows=n_lat, **common)
        else:
            xs, h = gated_residual(xs, f, gain(layer, 3), mods[layer], 5, gain_next=gain(layer + 1, 0),
                                   mods_next=mods[layer + 1],
                                   h_dtype=F32 if (layer + 1) % 2 else BF16, **common)
    return xs.reshape(nbatch, seq, d)
```
